```python
import jax, jax.numpy as jnp
from jax import lax
import numpy as np

D_MODEL = 1024
BATCH = 8
SEQ = 4096
DEPTH = 4
DEC_BATCH = 16
DEC_SEQ = 16
PAST_LEN = 4096

CHUNK = 64
EPS = 1e-6
N_EVEN = (DEPTH + 1) // 2
N_ODD = DEPTH // 2

A_HEADS = 4
A_EXPAND = 128
A_HEAD_DIM = D_MODEL // 8
A_KEY_WIDTH = A_HEADS * A_EXPAND
A_WIDTH = A_HEADS * A_HEAD_DIM
POOL_WINDOWS = (2, 4, 8, 16)
B_GROUPS = len(POOL_WINDOWS)
B_GROUP_DIM = D_MODEL // 8
B_WIDTH = B_GROUPS * B_GROUP_DIM
POOL_HIST = max(POOL_WINDOWS) - 1
AB_IN = 2 * A_KEY_WIDTH + 2 * A_WIDTH + B_WIDTH
AB_OUT = A_WIDTH + B_WIDTH
C_BLOCK = 128
C_GROUPS = 8
C_WIDTH = 2 * D_MODEL
C_GROUP_DIM = C_WIDTH // C_GROUPS
D_FF = 2816
CONV_W = 3

kernel_name = "hgrn2_pool_cmlp_streaming_step"


def rms_norm(x, gain):
    xf = x.astype(jnp.float32)
    y = xf * lax.rsqrt(jnp.mean(xf * xf, axis=-1, keepdims=True) + EPS)
    return (y * gain.astype(jnp.float32)).astype(x.dtype)


def layer_norm(x, gain, bias):
    xf = x.astype(jnp.float32)
    mu = jnp.mean(xf, axis=-1, keepdims=True)
    xc = xf - mu
    y = xc * lax.rsqrt(jnp.mean(xc * xc, axis=-1, keepdims=True) + EPS)
    return (y * gain.astype(jnp.float32) + bias.astype(jnp.float32)).astype(x.dtype)


def hgrn_lower_bounds(lb_logits):
    p = jax.nn.softmax(lb_logits.astype(jnp.float32), axis=0)
    c = jnp.cumsum(p, axis=0)
    return c - c[0:1]


def hgrn2_recurrence(q, k, log_f, v, s0):
    B, T, H, K = q.shape
    V = v.shape[-1]
    L = min(CHUNK, T)
    n = T // L

    def to_blocks(a):
        return a.astype(jnp.float32).reshape(B, n, L, H, a.shape[-1]).transpose(1, 0, 3, 2, 4)

    qc, kc, gc, vc = to_blocks(q), to_blocks(k), to_blocks(log_f), to_blocks(v)
    causal = jnp.tril(jnp.ones((L, L), dtype=bool))

    def step(S, inp):
        qb, kb, gb, vb = inp
        b = jnp.cumsum(gb, axis=2)
        rel = b[:, :, :, None, :] - b[:, :, None, :, :]
        decay = jnp.exp(jnp.where(causal[:, :, None], rel, -jnp.inf))
        scores = jnp.einsum('bhtk,bhsk,bhtsk->bhts', qb, kb, decay)
        o = (jnp.einsum('bhts,bhsv->bhtv', scores, vb)
             + jnp.einsum('bhtk,bhkv->bhtv', qb * jnp.exp(b), S))
        b_last = b[:, :, -1:, :]
        S_new = (jnp.exp(b_last[:, :, 0, :])[..., None] * S
                 + jnp.einsum('bhsk,bhsv->bhkv', kb * jnp.exp(b_last - b), vb))
        return S_new, o

    S_fin, oc = lax.scan(step, s0.astype(jnp.float32), (qc, kc, gc, vc))
    o = oc.transpose(1, 0, 3, 2, 4).reshape(B, T, H, V)
    return o, S_fin.astype(s0.dtype)


def multiscale_pool(p, hist, pos0, pool_w, pool_scale):
    B, T, _ = p.shape
    full = jnp.concatenate([hist.astype(p.dtype), p], axis=1).astype(jnp.float32)
    cs = jnp.concatenate([jnp.zeros((B, 1, B_WIDTH), jnp.float32), jnp.cumsum(full, axis=1)], axis=1)
    pos = pos0 + jnp.arange(T)
    cur = full[:, POOL_HIST:]
    outs = []
    for gi, w in enumerate(POOL_WINDOWS):
        sl = slice(gi * B_GROUP_DIM, (gi + 1) * B_GROUP_DIM)
        end = cs[:, POOL_HIST + 1:POOL_HIST + 1 + T, sl]
        start = cs[:, POOL_HIST + 1 - w:POOL_HIST + 1 - w + T, sl]
        cnt = jnp.minimum(pos + 1, w).astype(jnp.float32)[None, :, None]
        outs.append((end - start) / cnt - cur[:, :, sl])
    pooled = jnp.concatenate(outs, axis=-1).reshape(B, T, B_GROUPS, B_GROUP_DIM)
    y = jnp.einsum('btgc,gce->btge', pooled, pool_w.astype(jnp.float32)).reshape(B, T, B_WIDTH)
    y = y * pool_scale.astype(jnp.float32)
    return y.astype(p.dtype), full[:, -POOL_HIST:].astype(hist.dtype)


def ab_mixer(h, s_hgrn, pool_hist, pos0, lb, w_in, out_gain, pool_w, pool_scale, w_out):
    B, T, _ = h.shape
    proj = jnp.einsum('btd,de->bte', h, w_in)
    o1 = A_KEY_WIDTH
    o2 = 2 * A_KEY_WIDTH
    o3 = o2 + A_WIDTH
    o4 = o3 + A_WIDTH
    q, fz, i_in, g, p = proj[..., :o1], proj[..., o1:o2], proj[..., o2:o3], proj[..., o3:o4], proj[..., o4:]
    fz = fz.astype(jnp.float32)
    log_f = jnp.logaddexp(jnp.log(lb), jnp.log1p(-lb) + jax.nn.log_sigmoid(fz))
    k = (1.0 - lb) * jax.nn.sigmoid(-fz)
    q = jax.nn.silu(q.astype(jnp.float32)) * (A_EXPAND ** -0.5)
    heads = lambda a, d: a.reshape(B, T, A_HEADS, d)
    o, s_new = hgrn2_recurrence(heads(q, A_EXPAND), heads(k, A_EXPAND), heads(log_f, A_EXPAND),
                                heads(i_in, A_HEAD_DIM), s_hgrn)
    o = rms_norm(o, out_gain) * jax.nn.silu(heads(g, A_HEAD_DIM).astype(jnp.float32))
    o = o.reshape(B, T, A_WIDTH).astype(h.dtype)
    yb, pool_new = multiscale_pool(p, pool_hist, pos0, pool_w, pool_scale)
    mix = jnp.einsum('bte,ed->btd', jnp.concatenate([o, yb], axis=-1), w_out)
    return mix, s_new, pool_new


def chunk_mlp_mixer(h, w_uv, ln_gain, ln_bias, ws, bs, w_out):
    B, T, _ = h.shape
    z = jax.nn.gelu(jnp.einsum('btd,de->bte', h, w_uv), approximate=False)
    u, v = z[..., :C_WIDTH], z[..., C_WIDTH:]
    v = layer_norm(v, ln_gain, ln_bias)
    L = min(C_BLOCK, T)
    n = T // L
    idx = jnp.arange(L)
    mask = (idx[None, :] // CHUNK) <= (idx[:, None] // CHUNK)
    w = jnp.where(mask[None], ws[:, :L, :L], 0.0)
    vc = v.reshape(B, n, L, C_GROUPS, C_GROUP_DIM)
    mixed = (jnp.einsum('gts,bnsgc->bntgc', w, vc)
             + bs[:, :L].T[None, None, :, :, None])
    y = u * mixed.reshape(B, T, C_WIDTH).astype(u.dtype)
    return jnp.einsum('bte,ed->btd', y, w_out), v


def conv_ffn(h, conv_hist, w_in, conv_w, conv_b, w_out):
    T = h.shape[1]
    proj = jnp.einsum('btd,df->btf', h, w_in)
    a, gate = proj[..., :D_FF], proj[..., D_FF:]
    full = jnp.concatenate([conv_hist.astype(a.dtype), a], axis=1)
    conv = conv_b + sum(full[:, j:j + T] * conv_w[j] for j in range(CONV_W))
    y = jnp.einsum('btf,fd->btd', jax.nn.gelu(conv, approximate=False) * gate, w_out)
    return y, full[:, -(CONV_W - 1):].astype(conv_hist.dtype)


def trunk(x, s_hgrn, s_pool, s_conv, pos0, lbs, norm_mix, norm_ffn, norm_final, w_in_ab, hgrn_out_gain,
          pool_w, pool_scale, w_out_ab, w_uv_c, c_ln_gain, c_ln_bias, c_ws, c_bs, w_out_c,
          w_in_ffn, ffn_conv_w, ffn_conv_b, w_out_ffn):
    new_hgrn, new_pool, new_conv, new_cv = [], [], [], []
    for layer in range(DEPTH):
        h = rms_norm(x, norm_mix[layer])
        if layer % 2 == 0:
            e = layer // 2
            mix, sh, sp = ab_mixer(h, s_hgrn[e], s_pool[e], pos0, lbs[e], w_in_ab[e], hgrn_out_gain[e],
                                   pool_w[e], pool_scale[e], w_out_ab[e])
            new_hgrn.append(sh)
            new_pool.append(sp)
        else:
            o = layer // 2
            mix, v = chunk_mlp_mixer(h, w_uv_c[o], c_ln_gain[o], c_ln_bias[o], c_ws[o], c_bs[o], w_out_c[o])
            new_cv.append(v)
        x = x + mix.astype(x.dtype)
        h = rms_norm(x, norm_ffn[layer])
        f, sc = conv_ffn(h, s_conv[layer], w_in_ffn[layer], ffn_conv_w[layer], ffn_conv_b[layer], w_out_ffn[layer])
        new_conv.append(sc)
        x = x + f.astype(x.dtype)
    y = rms_norm(x, norm_final)
    return y, jnp.stack(new_hgrn), jnp.stack(new_pool), jnp.stack(new_conv), jnp.stack(new_cv)


def setup_inputs(seed: int = 0) -> dict:
    key = jax.random.key(seed)
    ks = jax.random.split(key, 24)
    f32 = jnp.float32

    def nrm(k, shape, scale):
        return scale * jax.random.normal(k, shape, f32)

    def gain(k, shape):
        return 1.0 + 0.05 * jax.random.normal(k, shape, f32)

    return {
        "x_prompt": nrm(ks[0], (BATCH, SEQ, D_MODEL), 1.0),
        "x_sample": nrm(ks[1], (DEC_BATCH, DEC_SEQ, D_MODEL), 1.0),
        "state_hgrn": nrm(ks[2], (N_EVEN, DEC_BATCH, A_HEADS, A_EXPAND, A_HEAD_DIM), 0.3),
        "state_pool": nrm(ks[3], (N_EVEN, DEC_BATCH, POOL_HIST, B_WIDTH), 1.0),
        "state_ffn_conv": nrm(ks[4], (DEPTH, DEC_BATCH, CONV_W - 1, D_FF), 1.0),
        "norm_mix": gain(ks[5], (DEPTH, D_MODEL)),
        "norm_ffn": gain(ks[6], (DEPTH, D_MODEL)),
        "norm_final": gain(ks[7], (D_MODEL,)),
        "hgrn_lb_logits": nrm(ks[8], (N_EVEN, A_KEY_WIDTH), 0.5),
        "w_in_ab": nrm(ks[9], (N_EVEN, D_MODEL, AB_IN), D_MODEL ** -0.5),
        "hgrn_out_gain": gain(ks[10], (N_EVEN, A_HEAD_DIM)),
        "pool_w": nrm(ks[11], (N_EVEN, B_GROUPS, B_GROUP_DIM, B_GROUP_DIM), B_GROUP_DIM ** -0.5),
        "pool_scale": gain(ks[12], (N_EVEN, B_WIDTH)),
        "w_out_ab": nrm(ks[13], (N_EVEN, AB_OUT, D_MODEL), AB_OUT ** -0.5),
        "w_uv_c": nrm(ks[14], (N_ODD, D_MODEL, 2 * C_WIDTH), D_MODEL ** -0.5),
        "c_ln_gain": gain(ks[15], (N_ODD, C_WIDTH)),
        "c_ln_bias": nrm(ks[16], (N_ODD, C_WIDTH), 0.02),
        "c_ws": nrm(ks[17], (N_ODD, C_GROUPS, C_BLOCK, C_BLOCK), C_BLOCK ** -0.5),
        "c_bs": gain(ks[18], (N_ODD, C_GROUPS, C_BLOCK)),
        "w_out_c": nrm(ks[19], (N_ODD, C_WIDTH, D_MODEL), C_WIDTH ** -0.5),
        "w_in_ffn": nrm(ks[20], (DEPTH, D_MODEL, 2 * D_FF), D_MODEL ** -0.5),
        "ffn_conv_w": nrm(ks[21], (DEPTH, CONV_W, D_FF), CONV_W ** -0.5),
        "ffn_conv_b": nrm(ks[22], (DEPTH, D_FF), 0.02),
        "w_out_ffn": nrm(ks[23], (DEPTH, D_FF, D_MODEL), D_FF ** -0.5),
    }


def reference(x_prompt, x_sample, state_hgrn, state_pool, state_ffn_conv, norm_mix, norm_ffn, norm_final,
              hgrn_lb_logits, w_in_ab, hgrn_out_gain, pool_w, pool_scale, w_out_ab, w_uv_c, c_ln_gain,
              c_ln_bias, c_ws, c_bs, w_out_c, w_in_ffn, ffn_conv_w, ffn_conv_b, w_out_ffn):
    lbs = hgrn_lower_bounds(hgrn_lb_logits)
    weights = (norm_mix, norm_ffn, norm_final, w_in_ab, hgrn_out_gain, pool_w, pool_scale, w_out_ab,
               w_uv_c, c_ln_gain, c_ln_bias, c_ws, c_bs, w_out_c, w_in_ffn, ffn_conv_w, ffn_conv_b, w_out_ffn)
    bp = x_prompt.shape[0]
    zero_hgrn = jnp.zeros((N_EVEN, bp, A_HEADS, A_EXPAND, A_HEAD_DIM), state_hgrn.dtype)
    zero_pool = jnp.zeros((N_EVEN, bp, POOL_HIST, B_WIDTH), state_pool.dtype)
    zero_conv = jnp.zeros((DEPTH, bp, CONV_W - 1, D_FF), state_ffn_conv.dtype)
    y_prompt, hgrn_prompt, pool_prompt, conv_prompt, _ = trunk(
        x_prompt, zero_hgrn, zero_pool, zero_conv, 0, lbs, *weights)
    y_sample, hgrn_sample, pool_sample, conv_sample, cmlp_v_sample = trunk(
        x_sample, state_hgrn, state_pool, state_ffn_conv, PAST_LEN, lbs, *weights)
    return (y_prompt, y_sample, hgrn_prompt, hgrn_sample, pool_prompt, pool_sample,
            conv_prompt, conv_sample, cmlp_v_sample)
```

```python
import functools

import jax
import jax.numpy as jnp
from jax import lax
from jax.experimental import pallas as pl
from jax.experimental.pallas import tpu as pltpu

F32 = jnp.float32
BF16 = jnp.bfloat16

D_MODEL = 1024
EPS = 1e-6
PAST_LEN = 4096
A_HEADS = 4
A_DIM = 128
A_WIDTH = A_HEADS * A_DIM
HGRN_CHUNK = 64
HGRN_SUB = 16
POOL_WINDOWS = (2, 4, 8, 16)
B_GROUP_DIM = 128
B_WIDTH = len(POOL_WINDOWS) * B_GROUP_DIM
POOL_ROWS = 16
AB_IN = 4 * A_WIDTH + B_WIDTH
C_BLOCK = 128
C_CAUSAL = 64
C_GROUPS = 8
C_WIDTH = 2 * D_MODEL
C_GROUP_DIM = C_WIDTH // C_GROUPS
D_FF = 2816
FF_CHUNK = 256
N_FF_CHUNKS = D_FF // FF_CHUNK
CONV_W = 3

PROMPT_TILE = 512
V7X_VMEM_LIMIT_BYTES = 56 * 1024 * 1024


def _dot(a, b):
    return jnp.dot(a, b, preferred_element_type=F32)


def _dot_nt(a, b):
    return lax.dot_general(a, b, (((1,), (1,)), ((), ())), preferred_element_type=F32)


def _dot_tn(a, b):
    return lax.dot_general(a, b, (((0,), (0,)), ((), ())), preferred_element_type=F32)


def _rms(x, gain):
    return x * lax.rsqrt(jnp.mean(x * x, axis=-1, keepdims=True) + EPS) * gain


def _sigmoid(x):
    return 1.0 / (1.0 + jnp.exp(-x))


def _gelu(x):
    return 0.5 * x * (1.0 + lax.erf(x * 0.7071067811865476))


def _bcast_rows(rows, n):
    return jnp.concatenate([jnp.broadcast_to(r, (n, r.shape[-1])) for r in rows], axis=0)


def _hgrn_chunk_head(qr, fz, v, state, log_lb, log1m_lb, one_m_lb, rm, eye):
    C = qr.shape[0]
    nb = C // HGRN_SUB
    q = qr * _sigmoid(qr) * (A_DIM ** -0.5)
    log_sig = jnp.minimum(fz, 0.0) - jnp.log1p(jnp.exp(-jnp.abs(fz)))
    y = log1m_lb + log_sig
    lf = jnp.maximum(log_lb, y) + jnp.log1p(jnp.exp(-jnp.abs(log_lb - y)))
    kk = one_m_lb * _sigmoid(-fz)

    al = lf
    for sh in (1, 2, 4, 8):
        al = al + jnp.where(rm >= sh, pltpu.roll(al, sh, 0), 0.0)
    tots = [jnp.sum(lf[j * HGRN_SUB:(j + 1) * HGRN_SUB], axis=0, keepdims=True) for j in range(nb)]
    tot_b = _bcast_rows(tots, HGRN_SUB)
    eq = q * jnp.exp(al)
    ek = kk * jnp.exp(jnp.minimum(tot_b - al, 0.0))
    if nb > 1:
        pre = [jnp.zeros_like(tots[0])]
        for j in range(nb - 1):
            pre.append(pre[-1] + tots[j])
        suf = [jnp.zeros_like(tots[0])]
        for j in range(nb - 1, 0, -1):
            suf.insert(0, suf[0] + tots[j])
        total = pre[-1] + tots[-1]
        qs = eq * jnp.exp(_bcast_rows(pre, HGRN_SUB))
        ks = ek * jnp.exp(_bcast_rows(suf, HGRN_SUB))
    else:
        total = tots[0]
        qs, ks = eq, ek
    v_b = v.astype(BF16)
    eq_b = eq.astype(BF16)

    o_state = _dot(qs.astype(BF16), state.astype(BF16))
    dec_row = jnp.exp(total)
    dec_col = jnp.sum(jnp.where(eye, jnp.broadcast_to(dec_row, (A_DIM, A_DIM)), 0.0), axis=1, keepdims=True)
    new_state = dec_col * state + _dot_tn(ks.astype(BF16), v_b)

    blocks = []
    kacc = None
    for i in range(nb):
        sl = slice(i * HGRN_SUB, (i + 1) * HGRN_SUB)
        acc = o_state[sl]
        if i > 0:
            p = _dot_nt(eq_b[sl], kacc.astype(BF16))
            acc = acc + _dot(p.astype(BF16), v_b[:i * HGRN_SUB])
        blocks.append(acc)
        if i < nb - 1:
            kacc = ek[sl] if kacc is None else jnp.concatenate([kacc * jnp.exp(tots[i]), ek[sl]], axis=0)
    o = blocks[0] if nb == 1 else jnp.concatenate(blocks, axis=0)

    for d0 in range(8):
        k1 = kk if d0 == 0 else pltpu.roll(kk, d0, 0)
        a1 = al if d0 == 0 else pltpu.roll(al, d0, 0)
        v1 = v if d0 == 0 else pltpu.roll(v, d0, 0)
        for d in (d0, d0 + 8):
            if d == d0:
                k2, a2, v2 = k1, a1, v1
            else:
                k2, a2, v2 = pltpu.roll(k1, 8, 0), pltpu.roll(a1, 8, 0), pltpu.roll(v1, 8, 0)
            if d == 0:
                w = jnp.sum(q * k2, axis=-1, keepdims=True)
            else:
                w = jnp.sum(q * k2 * jnp.exp(jnp.minimum(al - a2, 0.0)), axis=-1, keepdims=True)
                w = jnp.where(rm >= d, w, 0.0)
            o = o + w * v2
    return o, new_state


def _even_body(layer_e, n_even, nseq, L, C, pos0, has_state, *refs):
    if has_state:
        (x_ref, sin_ref, pin_ref, gain_ref, lbl_ref, win_ref, og_ref, pw_ref, ps_ref, wout_ref,
         xo_ref, sout_ref, pout_ref, proj_scr, o_scr, ext_scr, s_scr, ph_scr) = refs
    else:
        (x_ref, gain_ref, lbl_ref, win_ref, og_ref, pw_ref, ps_ref, wout_ref,
         xo_ref, sout_ref, pout_ref, proj_scr, o_scr, ext_scr, s_scr, ph_scr) = refs
    t = pl.program_id(1)
    Tt = nseq * L
    seg_rows = POOL_ROWS + L

    @pl.when(t == 0)
    def _():
        if has_state:
            s_scr[...] = sin_ref[...]
            ph_scr[...] = pin_ref[...]
        else:
            s_scr[...] = jnp.zeros_like(s_scr)
            ph_scr[...] = jnp.zeros_like(ph_scr)

    x = x_ref[...]
    h = _rms(x, gain_ref[...]).astype(BF16)
    proj_scr[...] = _dot(h, win_ref[...])

    lrows = [lbl_ref[pl.ds(i, 1), :] for i in range(n_even)]
    lmax = functools.reduce(jnp.maximum, lrows)
    lexp = [jnp.exp(r - lmax) for r in lrows]
    lden = functools.reduce(lambda a, b: a + b, lexp)
    lb = jnp.zeros_like(lden)
    for i in range(1, layer_e + 1):
        lb = lb + lexp[i] / lden
    log_lb = jnp.log(lb)
    log1m_lb = jnp.log1p(-lb)
    one_m_lb = 1.0 - lb

    rm = lax.broadcasted_iota(jnp.int32, (C, 1), 0) & (HGRN_SUB - 1)
    eye = (lax.broadcasted_iota(jnp.int32, (A_DIM, A_DIM), 0)
           == lax.broadcasted_iota(jnp.int32, (A_DIM, A_DIM), 1))
    chunks_per_seg = L // C

    def chunk(i, carry):
        row0 = pl.multiple_of(i * C, C)
        seg = i // chunks_per_seg if nseq > 1 else 0
        for hh in range(A_HEADS):
            c0 = hh * A_DIM
            qr = proj_scr[pl.ds(row0, C), c0:c0 + A_DIM]
            fz = proj_scr[pl.ds(row0, C), A_WIDTH + c0:A_WIDTH + c0 + A_DIM]
            v = proj_scr[pl.ds(row0, C), 2 * A_WIDTH + c0:2 * A_WIDTH + c0 + A_DIM]
            o, new_state = _hgrn_chunk_head(
                qr, fz, v, s_scr[seg, hh], log_lb[:, c0:c0 + A_DIM], log1m_lb[:, c0:c0 + A_DIM],
                one_m_lb[:, c0:c0 + A_DIM], rm, eye)
            o_scr[pl.ds(row0, C), c0:c0 + A_DIM] = o
            s_scr[seg, hh] = new_state
        return carry

    lax.fori_loop(0, Tt // C, chunk, 0)

    for s in range(nseq):
        ext_scr[pl.ds(s * seg_rows, POOL_ROWS), :] = ph_scr[s]
        ext_scr[pl.ds(s * seg_rows + POOL_ROWS, L), :] = proj_scr[pl.ds(s * L, L), 4 * A_WIDTH:AB_IN]
    for s in range(nseq):
        ph_scr[s] = ext_scr[pl.ds(s * seg_rows + L, POOL_ROWS), :]

    def new_rows(a):
        if nseq == 1:
            return a[POOL_ROWS:]
        return a.reshape(nseq, seg_rows, a.shape[-1])[:, POOL_ROWS:, :].reshape(Tt, a.shape[-1])

    pos = pos0 + t * L + (lax.broadcasted_iota(jnp.int32, (Tt, 1), 0) & (L - 1))
    mix = None
    og = og_ref[...]
    for hh in range(A_HEADS):
        c0 = hh * A_DIM
        oh = o_scr[:, c0:c0 + A_DIM]
        gate = proj_scr[:, 3 * A_WIDTH + c0:3 * A_WIDTH + c0 + A_DIM]
        oh = _rms(oh, og) * (gate * _sigmoid(gate))
        part = _dot(oh.astype(BF16), wout_ref[c0:c0 + A_DIM, :])
        mix = part if mix is None else mix + part
    for gi, w in enumerate(POOL_WINDOWS):
        c0 = gi * B_GROUP_DIM
        cur = ext_scr[:, c0:c0 + B_GROUP_DIM]
        acc = cur
        sh = 1
        while sh < w:
            acc = acc + pltpu.roll(acc, sh, 0)
            sh *= 2
        cnt = jnp.minimum(pos + 1, w).astype(F32)
        pooled = new_rows(acc) / cnt - new_rows(cur)
        yb = _dot(pooled.astype(BF16), pw_ref[gi]) * ps_ref[:, c0:c0 + B_GROUP_DIM]
        mix = mix + _dot(yb.astype(BF16), wout_ref[A_WIDTH + c0:A_WIDTH + c0 + B_GROUP_DIM, :])

    xo_ref[...] = x + mix
    sout_ref[...] = s_scr[...]
    pout_ref[...] = ph_scr[...]


def _full_spec(shape):
    zeros = (0,) * len(shape)
    return pl.BlockSpec(shape, lambda g, t: zeros, pipeline_mode=pl.Buffered(1))


def _even_layer(layer_e, n_even, x, s_hgrn, s_pool, gain, lb_logits, w_in, out_gain, pool_w, pool_scale, w_out,
                *, nseq, L, C, pos0):
    G, T, _ = x.shape
    Tt = nseq * L
    has_state = s_hgrn is not None
    body = functools.partial(_even_body, layer_e, n_even, nseq, L, C, pos0, has_state)
    x_spec = pl.BlockSpec((None, Tt, D_MODEL), lambda g, t: (g, t, 0))
    s_spec = pl.BlockSpec((nseq, A_HEADS, A_DIM, A_DIM), lambda g, t: (g, 0, 0, 0))
    p_spec = pl.BlockSpec((nseq, POOL_ROWS, B_WIDTH), lambda g, t: (g, 0, 0))
    weights = (gain, lb_logits, w_in, out_gain, pool_w, pool_scale, w_out)
    w_specs = [_full_spec(w.shape) for w in weights]
    state_in = (s_hgrn, s_pool) if has_state else ()
    state_specs = [s_spec, p_spec] if has_state else []
    return pl.pallas_call(
        body,
        grid=(G, T // Tt),
        in_specs=[x_spec] + state_specs + w_specs,
        out_specs=(x_spec, s_spec, p_spec),
        out_shape=(jax.ShapeDtypeStruct(x.shape, F32),
                   jax.ShapeDtypeStruct((G * nseq, A_HEADS, A_DIM, A_DIM), F32),
                   jax.ShapeDtypeStruct((G * nseq, POOL_ROWS, B_WIDTH), F32)),
        scratch_shapes=[pltpu.VMEM((Tt, AB_IN), F32), pltpu.VMEM((Tt, A_WIDTH), F32),
                        pltpu.VMEM((nseq * (POOL_ROWS + L), B_WIDTH), F32),
                        pltpu.VMEM((nseq, A_HEADS, A_DIM, A_DIM), F32),
                        pltpu.VMEM((nseq, POOL_ROWS, B_WIDTH), F32)],
        compiler_params=pltpu.CompilerParams(dimension_semantics=("arbitrary", "arbitrary"),
                                             vmem_limit_bytes=V7X_VMEM_LIMIT_BYTES),
        name=f"even_mixer_{layer_e}_{'s' if has_state else 'p'}",
    )(x, *state_in, *weights)


def _odd_body(Tt, Lc, want_v, *refs):
    if want_v:
        (x_ref, gain_ref, wu_ref, wv_ref, lng_ref, lnb_ref, wmix_ref, bias_ref, wout_ref,
         xo_ref, vo_ref, vn_scr, acc_scr) = refs
    else:
        (x_ref, gain_ref, wu_ref, wv_ref, lng_ref, lnb_ref, wmix_ref, bias_ref, wout_ref,
         xo_ref, vn_scr, acc_scr) = refs
    x = x_ref[...]
    h = _rms(x, gain_ref[...]).astype(BF16)
    zv = _gelu(_dot(h, wv_ref[...]))
    mu = jnp.mean(zv, axis=-1, keepdims=True)
    zc = zv - mu
    vn = zc * lax.rsqrt(jnp.mean(zc * zc, axis=-1, keepdims=True) + EPS) * lng_ref[...] + lnb_ref[...]
    if want_v:
        vo_ref[...] = vn
    vn_scr[...] = vn.astype(BF16)
    for g in range(C_GROUPS):
        c0 = g * C_GROUP_DIM
        u = _gelu(_dot(h, wu_ref[g]))
        bias = bias_ref[:, g:g + 1]
        mixed = [_dot(wmix_ref[g], vn_scr[pl.ds(b * Lc, Lc), c0:c0 + C_GROUP_DIM]) + bias
                 for b in range(Tt // Lc)]
        mixed = mixed[0] if len(mixed) == 1 else jnp.concatenate(mixed, axis=0)
        part = _dot((u * mixed).astype(BF16), wout_ref[g])
        if g == 0:
            acc_scr[...] = part
        else:
            acc_scr[...] += part
    xo_ref[...] = x + acc_scr[...]


def _odd_layer(x, gain, w_u, w_v, ln_gain, ln_bias, w_mix, bias, w_out, *, Tt, want_v):
    G, T, _ = x.shape
    Lc = w_mix.shape[-1]
    body = functools.partial(_odd_body, Tt, Lc, want_v)
    x_spec = pl.BlockSpec((None, Tt, D_MODEL), lambda g, t: (g, t, 0))
    weights = (gain, w_u, w_v, ln_gain, ln_bias, w_mix, bias, w_out)
    out_specs = [x_spec]
    out_shape = [jax.ShapeDtypeStruct(x.shape, F32)]
    if want_v:
        out_specs.append(pl.BlockSpec((None, Tt, C_WIDTH), lambda g, t: (g, t, 0)))
        out_shape.append(jax.ShapeDtypeStruct((G, T, C_WIDTH), F32))
    return pl.pallas_call(
        body,
        grid=(G, T // Tt),
        in_specs=[x_spec] + [_full_spec(w.shape) for w in weights],
        out_specs=tuple(out_specs),
        out_shape=tuple(out_shape),
        scratch_shapes=[pltpu.VMEM((Tt, C_WIDTH), BF16), pltpu.VMEM((Tt, D_MODEL), F32)],
        compiler_params=pltpu.CompilerParams(dimension_semantics=("arbitrary", "arbitrary"),
                                             vmem_limit_bytes=V7X_VMEM_LIMIT_BYTES),
        name=f"odd_mixer_{'s' if want_v else 'p'}",
    )(x, *weights)


def _ffn_body(nseq, L, has_state, final_norm, *refs):
    refs = list(refs)
    x_ref = refs.pop(0)
    hin_ref = refs.pop(0) if has_state else None
    gain_ref, wa_ref, wg_ref, cw_ref, cb_ref, wd_ref = refs[:6]
    refs = refs[6:]
    gf_ref = refs.pop(0) if final_norm else None
    xo_ref, hout_ref, acc_scr, hist_scr, a_scr = refs
    t = pl.program_id(1)
    Tt = nseq * L

    @pl.when(t == 0)
    def _():
        if has_state:
            hist_scr[0] = hin_ref[:, 0, :]
            hist_scr[1] = hin_ref[:, 1, :]
        else:
            hist_scr[...] = jnp.zeros_like(hist_scr)

    x = x_ref[...]
    h = _rms(x, gain_ref[...]).astype(BF16)
    rowm = lax.broadcasted_iota(jnp.int32, (Tt, 1), 0) & (L - 1)

    def expand(rows):
        if nseq == 1:
            return rows
        return jnp.broadcast_to(rows[:, None, :], (nseq, L, rows.shape[-1])).reshape(Tt, rows.shape[-1])

    def last_row(k):
        return a_scr[:, L - k, :]

    acc_scr[...] = jnp.zeros_like(acc_scr)

    def chunk(j, carry):
        c0 = pl.multiple_of(j * FF_CHUNK, FF_CHUNK)
        a = _dot(h, wa_ref[j])
        gate = _dot(h, wg_ref[j])
        h2 = expand(hist_scr[0, :, pl.ds(c0, FF_CHUNK)])
        h1 = expand(hist_scr[1, :, pl.ds(c0, FF_CHUNK)])
        a1 = jnp.where(rowm >= 1, pltpu.roll(a, 1, 0), h1)
        a2 = jnp.where(rowm >= 2, pltpu.roll(a, 2, 0), jnp.where(rowm == 1, h1, h2))
        conv = (cb_ref[j] + a2 * cw_ref[j, pl.ds(0, 1), :] + a1 * cw_ref[j, pl.ds(1, 1), :]
                + a * cw_ref[j, pl.ds(2, 1), :])
        y = (_gelu(conv) * gate).astype(BF16)
        acc_scr[...] += _dot(y, wd_ref[j])
        a_scr[...] = a.reshape(nseq, L, FF_CHUNK)
        hist_scr[0, :, pl.ds(c0, FF_CHUNK)] = last_row(2)
        hist_scr[1, :, pl.ds(c0, FF_CHUNK)] = last_row(1)
        return carry

    lax.fori_loop(0, N_FF_CHUNKS, chunk, 0)
    xn = x + acc_scr[...]
    if final_norm:
        xn = _rms(xn, gf_ref[...])
    xo_ref[...] = xn
    hout_ref[:, 0, :] = hist_scr[0]
    hout_ref[:, 1, :] = hist_scr[1]


def _ffn_layer(x, s_conv, gain, w_a, w_g, conv_w, conv_b, w_d, final_gain, *, nseq, L):
    G, T, _ = x.shape
    Tt = nseq * L
    has_state = s_conv is not None
    final_norm = final_gain is not None
    body = functools.partial(_ffn_body, nseq, L, has_state, final_norm)
    x_spec = pl.BlockSpec((None, Tt, D_MODEL), lambda g, t: (g, t, 0))
    h_spec = pl.BlockSpec((nseq, CONV_W - 1, D_FF), lambda g, t: (g, 0, 0))
    weights = (gain, w_a, w_g, conv_w, conv_b, w_d) + ((final_gain,) if final_norm else ())
    state_in = (s_conv,) if has_state else ()
    return pl.pallas_call(
        body,
        grid=(G, T // Tt),
        in_specs=[x_spec] + ([h_spec] if has_state else []) + [_full_spec(w.shape) for w in weights],
        out_specs=(x_spec, h_spec),
        out_shape=(jax.ShapeDtypeStruct(x.shape, F32),
                   jax.ShapeDtypeStruct((G * nseq, CONV_W - 1, D_FF), F32)),
        scratch_shapes=[pltpu.VMEM((Tt, D_MODEL), F32), pltpu.VMEM((CONV_W - 1, nseq, D_FF), F32),
                        pltpu.VMEM((nseq, L, FF_CHUNK), F32)],
        compiler_params=pltpu.CompilerParams(dimension_semantics=("arbitrary", "arbitrary"),
                                             vmem_limit_bytes=V7X_VMEM_LIMIT_BYTES),
        name=f"conv_ffn_{'s' if has_state else 'p'}{'_final' if final_norm else ''}",
    )(x, *state_in, *weights)


def _trunk(x, s_hgrn, s_pool, s_conv, pos0, w, *, nseq, L, want_v):
    depth = w["norm_mix"].shape[0]
    n_even = w["lb_logits"].shape[0]
    Tt = nseq * L
    C = min(HGRN_CHUNK, L)
    new_hgrn, new_pool, new_conv, new_cv = [], [], [], []
    for layer in range(depth):
        if layer % 2 == 0:
            e = layer // 2
            x, sh, sp = _even_layer(
                e, n_even, x, None if s_hgrn is None else s_hgrn[e], None if s_pool is None else s_pool[e],
                w["norm_mix"][layer][None], w["lb_logits"], w["w_in_ab"][e], w["hgrn_out_gain"][e][None],
                w["pool_w"][e], w["pool_scale"][e][None], w["w_out_ab"][e], nseq=nseq, L=L, C=C, pos0=pos0)
            new_hgrn.append(sh)
            new_pool.append(sp)
        else:
            o = layer // 2
            out = _odd_layer(x, w["norm_mix"][layer][None], w["w_u"][o], w["w_v"][o], w["c_ln_gain"][o][None],
                             w["c_ln_bias"][o][None], w["w_mix"][o], w["c_bias"][o], w["w_out_c"][o],
                             Tt=Tt, want_v=want_v)
            x = out[0]
            if want_v:
                new_cv.append(out[1])
        final_gain = w["norm_final"][None] if layer == depth - 1 else None
        x, sc = _ffn_layer(x, None if s_conv is None else s_conv[layer], w["norm_ffn"][layer][None],
                           w["w_a"][layer], w["w_g"][layer], w["conv_w"][layer], w["conv_b"][layer],
                           w["w_d"][layer], final_gain, nseq=nseq, L=L)
        new_conv.append(sc)
    return x, jnp.stack(new_hgrn), jnp.stack(new_pool), jnp.stack(new_conv), new_cv


def kernel(x_prompt, x_sample, state_hgrn, state_pool, state_ffn_conv, norm_mix, norm_ffn, norm_final, hgrn_lb_logits, w_in_ab, hgrn_out_gain, pool_w, pool_scale, w_out_ab, w_uv_c, c_ln_gain, c_ln_bias, c_ws, c_bs, w_out_c, w_in_ffn, ffn_conv_w, ffn_conv_b, w_out_ffn):
    depth = norm_mix.shape[0]
    n_odd = w_uv_c.shape[0]
    bp, seq, _ = x_prompt.shape
    bs, dseq, _ = x_sample.shape

    def chunked_cols(a, n, width):
        lead = a.shape[:-2]
        k = a.shape[-2]
        a = a.reshape(lead + (k, n, width))
        return jnp.moveaxis(a, -2, -3)

    common = dict(
        norm_mix=norm_mix, norm_ffn=norm_ffn, norm_final=norm_final, lb_logits=hgrn_lb_logits,
        w_in_ab=w_in_ab.astype(BF16), hgrn_out_gain=hgrn_out_gain, pool_w=pool_w.astype(BF16),
        pool_scale=pool_scale, w_out_ab=w_out_ab.astype(BF16),
        w_u=chunked_cols(w_uv_c[:, :, :C_WIDTH].astype(BF16), C_GROUPS, C_GROUP_DIM),
        w_v=w_uv_c[:, :, C_WIDTH:].astype(BF16), c_ln_gain=c_ln_gain, c_ln_bias=c_ln_bias,
        w_out_c=w_out_c.astype(BF16).reshape(n_odd, C_GROUPS, C_GROUP_DIM, D_MODEL),
        w_a=chunked_cols(w_in_ffn[:, :, :D_FF].astype(BF16), N_FF_CHUNKS, FF_CHUNK),
        w_g=chunked_cols(w_in_ffn[:, :, D_FF:].astype(BF16), N_FF_CHUNKS, FF_CHUNK),
        conv_w=chunked_cols(ffn_conv_w, N_FF_CHUNKS, FF_CHUNK),
        conv_b=chunked_cols(ffn_conv_b[:, None, :], N_FF_CHUNKS, FF_CHUNK),
        w_d=w_out_ffn.astype(BF16).reshape(depth, N_FF_CHUNKS, FF_CHUNK, D_MODEL),
    )

    def gating(length, copies):
        idx = jnp.arange(length)
        mask = (idx[None, :] // C_CAUSAL) <= (idx[:, None] // C_CAUSAL)
        wm = jnp.where(mask[None, None], c_ws[:, :, :length, :length], 0.0)
        if copies > 1:
            eye = jnp.eye(copies, dtype=wm.dtype)
            wm = jnp.einsum("ab,ogts->ogatbs", eye, wm).reshape(n_odd, C_GROUPS, copies * length, copies * length)
        bias = jnp.tile(jnp.swapaxes(c_bs[:, :, :length], 1, 2), (1, copies, 1))
        return wm.astype(BF16), bias

    wm_p, bias_p = gating(min(C_BLOCK, seq), 1)
    Lp = min(PROMPT_TILE, seq)
    y_p, hg_p, pool_p, conv_p, _ = _trunk(x_prompt, None, None, None, 0, dict(common, w_mix=wm_p, c_bias=bias_p),
                                          nseq=1, L=Lp, want_v=False)

    wm_s, bias_s = gating(min(C_BLOCK, dseq), bs)
    pool_hist = jnp.pad(state_pool, ((0, 0), (0, 0), (POOL_ROWS - state_pool.shape[2], 0), (0, 0)))
    y_s, hg_s, pool_s, conv_s, cv_s = _trunk(
        x_sample.reshape(1, bs * dseq, D_MODEL), state_hgrn, pool_hist, state_ffn_conv, PAST_LEN,
        dict(common, w_mix=wm_s, c_bias=bias_s), nseq=bs, L=dseq, want_v=True)
    n_hist = state_pool.shape[2]
    cv_s = jnp.stack(cv_s).reshape(n_odd, bs, dseq, C_WIDTH)
    return (y_p, y_s.reshape(bs, dseq, D_MODEL), hg_p, hg_s, pool_p[:, :, POOL_ROWS - n_hist:],
            pool_s[:, :, POOL_ROWS - n_hist:], conv_p, conv_s, cv_s)
```

```python
import functools

import numpy as np
import jax
import jax.numpy as jnp
from jax import lax
from jax.experimental import pallas as pl
from jax.experimental.pallas import tpu as pltpu

F32 = jnp.float32
BF16 = jnp.bfloat16

D_MODEL = 1024
EPS = 1e-6
PAST_LEN = 4096
SUBLANES = 8
A_HEADS = 4
A_DIM = 128
A_WIDTH = A_HEADS * A_DIM
HGRN_CHUNK = 64
LOG2_E = 1.4426950408889634
POOL_WINDOWS = (2, 4, 8, 16)
B_GROUP_DIM = 128
B_WIDTH = len(POOL_WINDOWS) * B_GROUP_DIM
POOL_ROWS = 16
AB_IN = 4 * A_WIDTH + B_WIDTH
C_BLOCK = 128
C_CAUSAL = 64
C_GROUPS = 8
C_WIDTH = 2 * D_MODEL
C_GROUP_DIM = C_WIDTH // C_GROUPS
D_FF = 2816
FF_CHUNK = 256
N_FF_CHUNKS = D_FF // FF_CHUNK
CONV_W = 3

PROMPT_TILE = 512
V7X_VMEM_LIMIT_BYTES = 56 * 1024 * 1024


def _dot(a, b):
    return jnp.dot(a, b, preferred_element_type=F32)


def _dot_nt(a, b):
    return lax.dot_general(a, b, (((1,), (1,)), ((), ())), preferred_element_type=F32)


def _dot_tn(a, b):
    return lax.dot_general(a, b, (((0,), (0,)), ((), ())), preferred_element_type=F32)


def _rms(x, gain):
    return x * lax.rsqrt(jnp.mean(x * x, axis=-1, keepdims=True) + EPS) * gain


def _sigmoid(x):
    return 1.0 / (1.0 + jnp.exp(-x))


def _gelu(x):
    return 0.5 * x * (1.0 + lax.erf(x * 0.7071067811865476))


def _hgrn_levels(C):
    levels, size = [], 2
    while size <= C:
        levels.append(size)
        size *= 2
    return levels


def _hgrn_sum_matrix(C):
    t = np.arange(C)[:, None]
    s = np.arange(C)[None, :]
    blocks = [s <= t, s > t]
    for size in _hgrn_levels(C):
        half = size // 2
        same = (t // size) == (s // size)
        blocks.append((same & (t % size >= half) & (s % size >= half) & (s <= t))
                      | (same & (t % size < half) & (s % size < half) & (s > t)))
    blocks.append(np.ones((SUBLANES, C), bool))
    m = np.concatenate(blocks, axis=0).astype(np.float32)
    return np.concatenate([m, m, m], axis=1)


def _even_body(layer_e, n_even, nseq, L, C, pos0, has_state, *refs):
    if has_state:
        (x_ref, sin_ref, pin_ref, gain_ref, lbl_ref, win_ref, og_ref, pw_ref, ps_ref, wout_ref, m3_ref,
         xo_ref, sout_ref, pout_ref, proj_scr, o_scr, ext_scr, q_scr, k_scr, e_scr, p_scr, s_scr, ph_scr) = refs
    else:
        (x_ref, gain_ref, lbl_ref, win_ref, og_ref, pw_ref, ps_ref, wout_ref, m3_ref,
         xo_ref, sout_ref, pout_ref, proj_scr, o_scr, ext_scr, q_scr, k_scr, e_scr, p_scr, s_scr, ph_scr) = refs
    t = pl.program_id(1)
    Tt = nseq * L
    seg_rows = POOL_ROWS + L
    levels = _hgrn_levels(C)
    e_rows = e_scr.shape[1]

    @pl.when(t == 0)
    def _():
        if has_state:
            for s in range(nseq):
                for hh in range(A_HEADS):
                    s_scr[s, hh] = sin_ref[s, hh].T
            ph_scr[...] = pin_ref[...]
        else:
            s_scr[...] = jnp.zeros_like(s_scr)
            ph_scr[...] = jnp.zeros_like(ph_scr)

    x = x_ref[...]
    h = _rms(x, gain_ref[...]).astype(BF16)
    proj_scr[...] = _dot(h, win_ref[...])

    lrows = [lbl_ref[pl.ds(i, 1), :] for i in range(n_even)]
    lmax = functools.reduce(jnp.maximum, lrows)
    lexp = [jnp.exp(r - lmax) for r in lrows]
    lden = functools.reduce(lambda a, b: a + b, lexp)
    lb = jnp.zeros_like(lden)
    for i in range(1, layer_e + 1):
        lb = lb + lexp[i] / lden
    log_lb = jnp.log(lb)
    log1m_lb = jnp.log1p(-lb)
    one_m_lb = 1.0 - lb

    qi = lax.broadcasted_iota(jnp.int32, (C, C), 0)
    ki = lax.broadcasted_iota(jnp.int32, (C, C), 1)
    diagonal = qi == ki
    owned = {size: ((qi // size) == (ki // size)) & ((qi & (size - 1)) >= size // 2) & ((ki & (size - 1)) < size // 2)
             for size in levels}
    chunks_per_seg = L // C

    n_chunks = Tt // C
    heads = [slice(hh * A_DIM, (hh + 1) * A_DIM) for hh in range(A_HEADS)]

    qr = proj_scr[:, 0:A_WIDTH]
    fz = proj_scr[:, A_WIDTH:2 * A_WIDTH]
    q_scr[...] = qr * _sigmoid(qr) * (A_DIM ** -0.5)
    log_sig = jnp.minimum(fz, 0.0) - jnp.log(1.0 + jnp.exp(-jnp.abs(fz)))
    y = log1m_lb + log_sig
    lf = jnp.maximum(log_lb, y) + jnp.log(1.0 + jnp.exp(-jnp.abs(log_lb - y)))
    k_scr[...] = one_m_lb * _sigmoid(-fz)
    lf2 = lf * LOG2_E
    hi = lf2.astype(BF16)
    r1 = lf2 - hi.astype(F32)
    mid = r1.astype(BF16)
    lo = (r1 - mid.astype(F32)).astype(BF16)
    for c in range(n_chunks):
        rs = slice(c * C, (c + 1) * C)
        e_scr[c] = _dot(m3_ref[...], jnp.concatenate([hi[rs], mid[rs], lo[rs]], axis=0))

    def scores(c):
        rs = slice(c * C, (c + 1) * C)
        for hh, cs in enumerate(heads):
            q, kk = q_scr[rs, cs], k_scr[rs, cs]
            p = jnp.where(diagonal, _dot_nt(q.astype(BF16), kk.astype(BF16)), 0.0)
            for li, size in enumerate(levels):
                r0 = (2 + li) * C
                decay = jnp.exp2(e_scr[c, r0:r0 + C, cs])
                p = jnp.where(owned[size], _dot_nt((q * decay).astype(BF16), (kk * decay).astype(BF16)), p)
            p_scr[c % 2, hh] = p.astype(BF16)

    def carried(c):
        rs = slice(c * C, (c + 1) * C)
        seg = c // chunks_per_seg
        for hh, cs in enumerate(heads):
            state_t = s_scr[seg, hh]
            qs = (q_scr[rs, cs] * jnp.exp2(e_scr[c, 0:C, cs])).astype(BF16)
            ks = (k_scr[rs, cs] * jnp.exp2(e_scr[c, C:2 * C, cs])).astype(BF16)
            o_scr[rs, cs] = _dot_nt(qs, state_t.astype(BF16))
            dec = jnp.exp2(e_scr[c, e_rows - SUBLANES:e_rows - SUBLANES + 1, cs])
            s_scr[seg, hh] = state_t * dec + _dot_tn(proj_scr[rs, 2 * A_WIDTH + hh * A_DIM:
                                                              2 * A_WIDTH + (hh + 1) * A_DIM].astype(BF16), ks)

    def weighted(c):
        rs = slice(c * C, (c + 1) * C)
        for hh, cs in enumerate(heads):
            v = proj_scr[rs, 2 * A_WIDTH + hh * A_DIM:2 * A_WIDTH + (hh + 1) * A_DIM]
            o_scr[rs, cs] += _dot(p_scr[c % 2, hh], v.astype(BF16))

    scores(0)
    for c in range(n_chunks):
        carried(c)
        if c + 1 < n_chunks:
            scores(c + 1)
        weighted(c)

    for s in range(nseq):
        ext_scr[pl.ds(s * seg_rows, POOL_ROWS), :] = ph_scr[s]
        ext_scr[pl.ds(s * seg_rows + POOL_ROWS, L), :] = proj_scr[pl.ds(s * L, L), 4 * A_WIDTH:AB_IN]
    for s in range(nseq):
        ph_scr[s] = ext_scr[pl.ds(s * seg_rows + L, POOL_ROWS), :]

    def new_rows(a):
        if nseq == 1:
            return a[POOL_ROWS:]
        return a.reshape(nseq, seg_rows, a.shape[-1])[:, POOL_ROWS:, :].reshape(Tt, a.shape[-1])

    pos = pos0 + t * L + (lax.broadcasted_iota(jnp.int32, (Tt, 1), 0) & (L - 1))
    mix = None
    og = og_ref[...]
    for hh in range(A_HEADS):
        c0 = hh * A_DIM
        oh = o_scr[:, c0:c0 + A_DIM]
        gate = proj_scr[:, 3 * A_WIDTH + c0:3 * A_WIDTH + c0 + A_DIM]
        oh = _rms(oh, og) * (gate * _sigmoid(gate))
        part = _dot(oh.astype(BF16), wout_ref[c0:c0 + A_DIM, :])
        mix = part if mix is None else mix + part
    for gi, w in enumerate(POOL_WINDOWS):
        c0 = gi * B_GROUP_DIM
        cur = ext_scr[:, c0:c0 + B_GROUP_DIM]
        acc = cur
        sh = 1
        while sh < w:
            acc = acc + pltpu.roll(acc, sh, 0)
            sh *= 2
        cnt = jnp.minimum(pos + 1, w).astype(F32)
        pooled = new_rows(acc) / cnt - new_rows(cur)
        yb = _dot(pooled.astype(BF16), pw_ref[gi]) * ps_ref[:, c0:c0 + B_GROUP_DIM]
        mix = mix + _dot(yb.astype(BF16), wout_ref[A_WIDTH + c0:A_WIDTH + c0 + B_GROUP_DIM, :])

    xo_ref[...] = x + mix
    for s in range(nseq):
        for hh in range(A_HEADS):
            sout_ref[s, hh] = s_scr[s, hh].T
    pout_ref[...] = ph_scr[...]


def _full_spec(shape):
    zeros = (0,) * len(shape)
    return pl.BlockSpec(shape, lambda g, t: zeros, pipeline_mode=pl.Buffered(1))


def _even_layer(layer_e, n_even, x, s_hgrn, s_pool, gain, lb_logits, w_in, out_gain, pool_w, pool_scale, w_out,
                *, nseq, L, C, pos0):
    G, T, _ = x.shape
    Tt = nseq * L
    has_state = s_hgrn is not None
    body = functools.partial(_even_body, layer_e, n_even, nseq, L, C, pos0, has_state)
    x_spec = pl.BlockSpec((None, Tt, D_MODEL), lambda g, t: (g, t, 0))
    s_spec = pl.BlockSpec((nseq, A_HEADS, A_DIM, A_DIM), lambda g, t: (g, 0, 0, 0))
    p_spec = pl.BlockSpec((nseq, POOL_ROWS, B_WIDTH), lambda g, t: (g, 0, 0))
    m3 = jnp.asarray(_hgrn_sum_matrix(C), BF16)
    weights = (gain, lb_logits, w_in, out_gain, pool_w, pool_scale, w_out, m3)
    w_specs = [_full_spec(w.shape) for w in weights]
    state_in = (s_hgrn, s_pool) if has_state else ()
    state_specs = [s_spec, p_spec] if has_state else []
    return pl.pallas_call(
        body,
        grid=(G, T // Tt),
        in_specs=[x_spec] + state_specs + w_specs,
        out_specs=(x_spec, s_spec, p_spec),
        out_shape=(jax.ShapeDtypeStruct(x.shape, F32),
                   jax.ShapeDtypeStruct((G * nseq, A_HEADS, A_DIM, A_DIM), F32),
                   jax.ShapeDtypeStruct((G * nseq, POOL_ROWS, B_WIDTH), F32)),
        scratch_shapes=[pltpu.VMEM((Tt, AB_IN), F32), pltpu.VMEM((Tt, A_WIDTH), F32),
                        pltpu.VMEM((nseq * (POOL_ROWS + L), B_WIDTH), F32),
                        pltpu.VMEM((Tt, A_WIDTH), F32), pltpu.VMEM((Tt, A_WIDTH), F32),
                        pltpu.VMEM((Tt // C, m3.shape[0], A_WIDTH), F32),
                        pltpu.VMEM((2, A_HEADS, C, C), BF16),
                        pltpu.VMEM((nseq, A_HEADS, A_DIM, A_DIM), F32),
                        pltpu.VMEM((nseq, POOL_ROWS, B_WIDTH), F32)],
        compiler_params=pltpu.CompilerParams(dimension_semantics=("arbitrary", "arbitrary"),
                                             vmem_limit_bytes=V7X_VMEM_LIMIT_BYTES),
        name=f"even_mixer_{layer_e}_{'s' if has_state else 'p'}",
    )(x, *state_in, *weights)


def _odd_body(Tt, Lc, want_v, *refs):
    if want_v:
        (x_ref, gain_ref, wu_ref, wv_ref, lng_ref, lnb_ref, wmix_ref, bias_ref, wout_ref,
         xo_ref, vo_ref, zv_scr, u_scr, vn_scr, m_scr, acc_scr) = refs
    else:
        (x_ref, gain_ref, wu_ref, wv_ref, lng_ref, lnb_ref, wmix_ref, bias_ref, wout_ref,
         xo_ref, zv_scr, u_scr, vn_scr, m_scr, acc_scr) = refs
    x = x_ref[...]
    h = _rms(x, gain_ref[...]).astype(BF16)
    zv_scr[...] = _dot(h, wv_ref[...])
    for g in range(C_GROUPS):
        u_scr[g] = _dot(h, wu_ref[g])
    zv = _gelu(zv_scr[...])
    mu = jnp.mean(zv, axis=-1, keepdims=True)
    zc = zv - mu
    vn = zc * lax.rsqrt(jnp.mean(zc * zc, axis=-1, keepdims=True) + EPS) * lng_ref[...] + lnb_ref[...]
    if want_v:
        vo_ref[...] = vn
    vn_scr[...] = vn.astype(BF16)

    def gate_rows(g):
        c0 = g * C_GROUP_DIM
        bias = bias_ref[:, g:g + 1]
        for b in range(Tt // Lc):
            m_scr[g % 2, pl.ds(b * Lc, Lc), :] = (
                _dot(wmix_ref[g], vn_scr[pl.ds(b * Lc, Lc), c0:c0 + C_GROUP_DIM]) + bias)

    gate_rows(0)
    for g in range(C_GROUPS):
        if g + 1 < C_GROUPS:
            gate_rows(g + 1)
        y = (_gelu(u_scr[g]) * m_scr[g % 2]).astype(BF16)
        part = _dot(y, wout_ref[g])
        if g == 0:
            acc_scr[...] = part
        else:
            acc_scr[...] += part
    xo_ref[...] = x + acc_scr[...]


def _odd_layer(x, gain, w_u, w_v, ln_gain, ln_bias, w_mix, bias, w_out, *, Tt, want_v):
    G, T, _ = x.shape
    Lc = w_mix.shape[-1]
    body = functools.partial(_odd_body, Tt, Lc, want_v)
    x_spec = pl.BlockSpec((None, Tt, D_MODEL), lambda g, t: (g, t, 0))
    weights = (gain, w_u, w_v, ln_gain, ln_bias, w_mix, bias, w_out)
    out_specs = [x_spec]
    out_shape = [jax.ShapeDtypeStruct(x.shape, F32)]
    if want_v:
        out_specs.append(pl.BlockSpec((None, Tt, C_WIDTH), lambda g, t: (g, t, 0)))
        out_shape.append(jax.ShapeDtypeStruct((G, T, C_WIDTH), F32))
    return pl.pallas_call(
        body,
        grid=(G, T // Tt),
        in_specs=[x_spec] + [_full_spec(w.shape) for w in weights],
        out_specs=tuple(out_specs),
        out_shape=tuple(out_shape),
        scratch_shapes=[pltpu.VMEM((Tt, C_WIDTH), F32), pltpu.VMEM((C_GROUPS, Tt, C_GROUP_DIM), F32),
                        pltpu.VMEM((Tt, C_WIDTH), BF16), pltpu.VMEM((2, Tt, C_GROUP_DIM), F32),
                        pltpu.VMEM((Tt, D_MODEL), F32)],
        compiler_params=pltpu.CompilerParams(dimension_semantics=("arbitrary", "arbitrary"),
                                             vmem_limit_bytes=V7X_VMEM_LIMIT_BYTES),
        name=f"odd_mixer_{'s' if want_v else 'p'}",
    )(x, *weights)


def _ffn_body(nseq, L, has_state, final_norm, *refs):
    refs = list(refs)
    x_ref = refs.pop(0)
    hin_ref = refs.pop(0) if has_state else None
    gain_ref, wag_ref, cw_ref, cb_ref, wd_ref = refs[:5]
    refs = refs[5:]
    gf_ref = refs.pop(0) if final_norm else None
    xo_ref, hout_ref, acc_scr, hist_scr, ag_scr = refs
    t = pl.program_id(1)
    Tt = nseq * L

    @pl.when(t == 0)
    def _():
        if has_state:
            hist_scr[0] = hin_ref[:, 0, :]
            hist_scr[1] = hin_ref[:, 1, :]
        else:
            hist_scr[...] = jnp.zeros_like(hist_scr)

    x = x_ref[...]
    h = _rms(x, gain_ref[...]).astype(BF16)
    rowm = lax.broadcasted_iota(jnp.int32, (Tt, 1), 0) & (L - 1)

    def expand(rows):
        if nseq == 1:
            return rows
        return jnp.broadcast_to(rows[:, None, :], (nseq, L, rows.shape[-1])).reshape(Tt, rows.shape[-1])

    ag_scr[0] = _dot(h, wag_ref[0]).reshape(nseq, L, 2 * FF_CHUNK)
    for j in range(N_FF_CHUNKS):
        cols = slice(j * FF_CHUNK, (j + 1) * FF_CHUNK)
        slot = j % 2
        if j + 1 < N_FF_CHUNKS:
            ag_scr[1 - slot] = _dot(h, wag_ref[j + 1]).reshape(nseq, L, 2 * FF_CHUNK)
        a = ag_scr[slot, :, :, :FF_CHUNK].reshape(Tt, FF_CHUNK)
        gate = ag_scr[slot, :, :, FF_CHUNK:].reshape(Tt, FF_CHUNK)
        h2 = expand(hist_scr[0, :, cols])
        h1 = expand(hist_scr[1, :, cols])
        a1 = jnp.where(rowm >= 1, pltpu.roll(a, 1, 0), h1)
        a2 = jnp.where(rowm >= 2, pltpu.roll(a, 2, 0), jnp.where(rowm == 1, h1, h2))
        conv = (cb_ref[j] + a2 * cw_ref[j, pl.ds(0, 1), :] + a1 * cw_ref[j, pl.ds(1, 1), :]
                + a * cw_ref[j, pl.ds(2, 1), :])
        y = (_gelu(conv) * gate).astype(BF16)
        part = _dot(y, wd_ref[j])
        if j == 0:
            acc_scr[...] = part
        else:
            acc_scr[...] += part
        hist_scr[0, :, cols] = ag_scr[slot, :, L - 2, :FF_CHUNK]
        hist_scr[1, :, cols] = ag_scr[slot, :, L - 1, :FF_CHUNK]
    xn = x + acc_scr[...]
    if final_norm:
        xn = _rms(xn, gf_ref[...])
    xo_ref[...] = xn
    hout_ref[:, 0, :] = hist_scr[0]
    hout_ref[:, 1, :] = hist_scr[1]


def _ffn_layer(x, s_conv, gain, w_ag, conv_w, conv_b, w_d, final_gain, *, nseq, L):
    G, T, _ = x.shape
    Tt = nseq * L
    has_state = s_conv is not None
    final_norm = final_gain is not None
    body = functools.partial(_ffn_body, nseq, L, has_state, final_norm)
    x_spec = pl.BlockSpec((None, Tt, D_MODEL), lambda g, t: (g, t, 0))
    h_spec = pl.BlockSpec((nseq, CONV_W - 1, D_FF), lambda g, t: (g, 0, 0))
    weights = (gain, w_ag, conv_w, conv_b, w_d) + ((final_gain,) if final_norm else ())
    state_in = (s_conv,) if has_state else ()
    return pl.pallas_call(
        body,
        grid=(G, T // Tt),
        in_specs=[x_spec] + ([h_spec] if has_state else []) + [_full_spec(w.shape) for w in weights],
        out_specs=(x_spec, h_spec),
        out_shape=(jax.ShapeDtypeStruct(x.shape, F32),
                   jax.ShapeDtypeStruct((G * nseq, CONV_W - 1, D_FF), F32)),
        scratch_shapes=[pltpu.VMEM((Tt, D_MODEL), F32), pltpu.VMEM((CONV_W - 1, nseq, D_FF), F32),
                        pltpu.VMEM((2, nseq, L, 2 * FF_CHUNK), F32)],
        compiler_params=pltpu.CompilerParams(dimension_semantics=("arbitrary", "arbitrary"),
                                             vmem_limit_bytes=V7X_VMEM_LIMIT_BYTES),
        name=f"conv_ffn_{'s' if has_state else 'p'}{'_final' if final_norm else ''}",
    )(x, *state_in, *weights)


def _trunk(x, s_hgrn, s_pool, s_conv, pos0, w, *, nseq, L, want_v):
    depth = w["norm_mix"].shape[0]
    n_even = w["lb_logits"].shape[0]
    Tt = nseq * L
    C = min(HGRN_CHUNK, L)
    new_hgrn, new_pool, new_conv, new_cv = [], [], [], []
    for layer in range(depth):
        if layer % 2 == 0:
            e = layer // 2
            x, sh, sp = _even_layer(
                e, n_even, x, None if s_hgrn is None else s_hgrn[e], None if s_pool is None else s_pool[e],
                w["norm_mix"][layer][None], w["lb_logits"], w["w_in_ab"][e], w["hgrn_out_gain"][e][None],
                w["pool_w"][e], w["pool_scale"][e][None], w["w_out_ab"][e], nseq=nseq, L=L, C=C, pos0=pos0)
            new_hgrn.append(sh)
            new_pool.append(sp)
        else:
            o = layer // 2
            out = _odd_layer(x, w["norm_mix"][layer][None], w["w_u"][o], w["w_v"][o], w["c_ln_gain"][o][None],
                             w["c_ln_bias"][o][None], w["w_mix"][o], w["c_bias"][o], w["w_out_c"][o],
                             Tt=Tt, want_v=want_v)
            x = out[0]
            if want_v:
                new_cv.append(out[1])
        final_gain = w["norm_final"][None] if layer == depth - 1 else None
        x, sc = _ffn_layer(x, None if s_conv is None else s_conv[layer], w["norm_ffn"][layer][None],
                           w["w_ag"][layer], w["conv_w"][layer], w["conv_b"][layer],
                           w["w_d"][layer], final_gain, nseq=nseq, L=L)
        new_conv.append(sc)
    return x, jnp.stack(new_hgrn), jnp.stack(new_pool), jnp.stack(new_conv), new_cv


def kernel(x_prompt, x_sample, state_hgrn, state_pool, state_ffn_conv, norm_mix, norm_ffn, norm_final, hgrn_lb_logits, w_in_ab, hgrn_out_gain, pool_w, pool_scale, w_out_ab, w_uv_c, c_ln_gain, c_ln_bias, c_ws, c_bs, w_out_c, w_in_ffn, ffn_conv_w, ffn_conv_b, w_out_ffn):
    depth = norm_mix.shape[0]
    n_odd = w_uv_c.shape[0]
    bp, seq, _ = x_prompt.shape
    bs, dseq, _ = x_sample.shape

    def chunked_cols(a, n, width):
        lead = a.shape[:-2]
        k = a.shape[-2]
        a = a.reshape(lead + (k, n, width))
        return jnp.moveaxis(a, -2, -3)

    common = dict(
        norm_mix=norm_mix, norm_ffn=norm_ffn, norm_final=norm_final, lb_logits=hgrn_lb_logits,
        w_in_ab=w_in_ab.astype(BF16), hgrn_out_gain=hgrn_out_gain, pool_w=pool_w.astype(BF16),
        pool_scale=pool_scale, w_out_ab=w_out_ab.astype(BF16),
        w_u=chunked_cols(w_uv_c[:, :, :C_WIDTH].astype(BF16), C_GROUPS, C_GROUP_DIM),
        w_v=w_uv_c[:, :, C_WIDTH:].astype(BF16), c_ln_gain=c_ln_gain, c_ln_bias=c_ln_bias,
        w_out_c=w_out_c.astype(BF16).reshape(n_odd, C_GROUPS, C_GROUP_DIM, D_MODEL),
        w_ag=jnp.concatenate([chunked_cols(w_in_ffn[:, :, :D_FF].astype(BF16), N_FF_CHUNKS, FF_CHUNK),
                              chunked_cols(w_in_ffn[:, :, D_FF:].astype(BF16), N_FF_CHUNKS, FF_CHUNK)], axis=-1),
        conv_w=chunked_cols(ffn_conv_w, N_FF_CHUNKS, FF_CHUNK),
        conv_b=chunked_cols(ffn_conv_b[:, None, :], N_FF_CHUNKS, FF_CHUNK),
        w_d=w_out_ffn.astype(BF16).reshape(depth, N_FF_CHUNKS, FF_CHUNK, D_MODEL),
    )

    def gating(length, copies):
        idx = jnp.arange(length)
        mask = (idx[None, :] // C_CAUSAL) <= (idx[:, None] // C_CAUSAL)
        wm = jnp.where(mask[None, None], c_ws[:, :, :length, :length], 0.0)
        if copies > 1:
            eye = jnp.eye(copies, dtype=wm.dtype)
            wm = jnp.einsum("ab,ogts->ogatbs", eye, wm).reshape(n_odd, C_GROUPS, copies * length, copies * length)
        bias = jnp.tile(jnp.swapaxes(c_bs[:, :, :length], 1, 2), (1, copies, 1))
        return wm.astype(BF16), bias

    wm_p, bias_p = gating(min(C_BLOCK, seq), 1)
    Lp = min(PROMPT_TILE, seq)
    y_p, hg_p, pool_p, conv_p, _ = _trunk(x_prompt, None, None, None, 0, dict(common, w_mix=wm_p, c_bias=bias_p),
                                          nseq=1, L=Lp, want_v=False)

    wm_s, bias_s = gating(min(C_BLOCK, dseq), bs)
    pool_hist = jnp.pad(state_pool, ((0, 0), (0, 0), (POOL_ROWS - state_pool.shape[2], 0), (0, 0)))
    y_s, hg_s, pool_s, conv_s, cv_s = _trunk(
        x_sample.reshape(1, bs * dseq, D_MODEL), state_hgrn, pool_hist, state_ffn_conv, PAST_LEN,
        dict(common, w_mix=wm_s, c_bias=bias_s), nseq=bs, L=dseq, want_v=True)
    n_hist = state_pool.shape[2]
    cv_s = jnp.stack(cv_s).reshape(n_odd, bs, dseq, C_WIDTH)
    return (y_p, y_s.reshape(bs, dseq, D_MODEL), hg_p, hg_s, pool_p[:, :, POOL_ROWS - n_hist:],
            pool_s[:, :, POOL_ROWS - n_hist:], conv_p, conv_s, cv_s)
```

```python
import functools

import numpy as np
import jax
import jax.numpy as jnp
from jax import lax
from jax.experimental import pallas as pl
from jax.experimental.pallas import tpu as pltpu

F32 = jnp.float32
BF16 = jnp.bfloat16

D_MODEL = 1024
EPS = 1e-6
PAST_LEN = 4096
SUBLANES = 8
A_HEADS = 4
A_DIM = 128
A_WIDTH = A_HEADS * A_DIM
HGRN_CHUNK = 64
LOG2_E = 1.4426950408889634
POOL_WINDOWS = (2, 4, 8, 16)
B_GROUP_DIM = 128
B_WIDTH = len(POOL_WINDOWS) * B_GROUP_DIM
POOL_ROWS = 16
AB_IN = 4 * A_WIDTH + B_WIDTH
C_BLOCK = 128
C_CAUSAL = 64
C_GROUPS = 8
C_WIDTH = 2 * D_MODEL
C_GROUP_DIM = C_WIDTH // C_GROUPS
D_FF = 2816
FF_CHUNK = 256
N_FF_CHUNKS = D_FF // FF_CHUNK
CONV_W = 3

PROMPT_TILE = 512
V7X_VMEM_LIMIT_BYTES = 56 * 1024 * 1024


def _dot(a, b):
    return jnp.dot(a, b, preferred_element_type=F32)


def _dot_nt(a, b):
    return lax.dot_general(a, b, (((1,), (1,)), ((), ())), preferred_element_type=F32)


def _dot_tn(a, b):
    return lax.dot_general(a, b, (((0,), (0,)), ((), ())), preferred_element_type=F32)


def _rms(x, gain):
    return x * lax.rsqrt(jnp.mean(x * x, axis=-1, keepdims=True) + EPS) * gain


def _sigmoid(x):
    return 1.0 / (1.0 + jnp.exp(-x))


def _gelu(x):
    return 0.5 * x * (1.0 + lax.erf(x * 0.7071067811865476))


def _hgrn_levels(C):
    levels, size = [], 2
    while size <= C:
        levels.append(size)
        size *= 2
    return levels


def _hgrn_sum_matrix(C):
    t = np.arange(C)[:, None]
    s = np.arange(C)[None, :]
    blocks = [s <= t, s > t]
    for size in _hgrn_levels(C):
        half = size // 2
        same = (t // size) == (s // size)
        blocks.append((same & (t % size >= half) & (s % size >= half) & (s <= t))
                      | (same & (t % size < half) & (s % size < half) & (s > t)))
    blocks.append(np.ones((SUBLANES, C), bool))
    m = np.concatenate(blocks, axis=0).astype(np.float32)
    return np.concatenate([m, m, m], axis=1)


def _even_body(layer_e, n_even, nseq, L, C, pos0, has_state, *refs):
    if has_state:
        (x_ref, sin_ref, pin_ref, gain_ref, lbl_ref, win_ref, og_ref, pw_ref, ps_ref, wout_ref, m3_ref,
         xo_ref, sout_ref, pout_ref, proj_scr, o_scr, cat_scr, ext_scr, q_scr, k_scr, e_scr, p_scr, s_scr, ph_scr) = refs
    else:
        (x_ref, gain_ref, lbl_ref, win_ref, og_ref, pw_ref, ps_ref, wout_ref, m3_ref,
         xo_ref, sout_ref, pout_ref, proj_scr, o_scr, cat_scr, ext_scr, q_scr, k_scr, e_scr, p_scr, s_scr, ph_scr) = refs
    t = pl.program_id(1)
    Tt = nseq * L
    seg_rows = POOL_ROWS + L
    levels = _hgrn_levels(C)
    e_rows = e_scr.shape[1]

    @pl.when(t == 0)
    def _():
        if has_state:
            for s in range(nseq):
                for hh in range(A_HEADS):
                    s_scr[s, hh] = sin_ref[s, hh].T
            ph_scr[...] = pin_ref[...]
        else:
            s_scr[...] = jnp.zeros_like(s_scr)
            ph_scr[...] = jnp.zeros_like(ph_scr)

    x = x_ref[...]
    h = _rms(x, gain_ref[...]).astype(BF16)
    proj_scr[...] = _dot(h, win_ref[...])

    lrows = [lbl_ref[pl.ds(i, 1), :] for i in range(n_even)]
    lmax = functools.reduce(jnp.maximum, lrows)
    lexp = [jnp.exp(r - lmax) for r in lrows]
    lden = functools.reduce(lambda a, b: a + b, lexp)
    lb = jnp.zeros_like(lden)
    for i in range(1, layer_e + 1):
        lb = lb + lexp[i] / lden
    log_lb = jnp.log(lb)
    log1m_lb = jnp.log1p(-lb)
    one_m_lb = 1.0 - lb

    level_ids = [0] + levels
    if len(level_ids) % 2:
        level_ids.append(-1)
    level_pairs = [(level_ids[i], level_ids[i + 1]) for i in range(0, len(level_ids), 2)]
    qi = lax.broadcasted_iota(jnp.int32, (C, 2 * C), 0)
    kj = lax.broadcasted_iota(jnp.int32, (C, 2 * C), 1)
    ki = kj & (C - 1)

    def owns(size):
        if size == 0:
            return qi == ki
        return ((qi // size) == (ki // size)) & ((qi & (size - 1)) >= size // 2) & ((ki & (size - 1)) < size // 2)

    pair_owned = []
    for first, second in level_pairs:
        m = (kj < C) & owns(first)
        if second >= 0:
            m = m | ((kj >= C) & owns(second))
        pair_owned.append(m)
    chunks_per_seg = L // C

    n_chunks = Tt // C
    heads = [slice(hh * A_DIM, (hh + 1) * A_DIM) for hh in range(A_HEADS)]

    qr = proj_scr[:, 0:A_WIDTH]
    fz = proj_scr[:, A_WIDTH:2 * A_WIDTH]
    q_scr[...] = qr * _sigmoid(qr) * (A_DIM ** -0.5)
    log_sig = jnp.minimum(fz, 0.0) - jnp.log(1.0 + jnp.exp(-jnp.abs(fz)))
    y = log1m_lb + log_sig
    lf = jnp.maximum(log_lb, y) + jnp.log(1.0 + jnp.exp(-jnp.abs(log_lb - y)))
    k_scr[...] = one_m_lb * _sigmoid(-fz)
    lf2 = lf * LOG2_E
    hi = lf2.astype(BF16)
    r1 = lf2 - hi.astype(F32)
    mid = r1.astype(BF16)
    lo = (r1 - mid.astype(F32)).astype(BF16)
    for c in range(n_chunks):
        rs = slice(c * C, (c + 1) * C)
        e_scr[c] = _dot(m3_ref[...], jnp.concatenate([hi[rs], mid[rs], lo[rs]], axis=0))

    def scores(c):
        rs = slice(c * C, (c + 1) * C)
        zeros = jnp.zeros((C, A_DIM), BF16)
        for hh, cs in enumerate(heads):
            q, kk = q_scr[rs, cs], k_scr[rs, cs]

            def sides(size):
                if size <= 0:
                    return q.astype(BF16), kk.astype(BF16)
                r0 = (2 + levels.index(size)) * C
                decay = jnp.exp2(e_scr[c, r0:r0 + C, cs])
                return (q * decay).astype(BF16), (kk * decay).astype(BF16)

            p = None
            for (first, second), mask in zip(level_pairs, pair_owned):
                (q1, k1), (q2, k2) = sides(first), sides(second)
                lhs = jnp.concatenate([q1, q2], axis=1)
                rhs = jnp.concatenate([jnp.concatenate([k1, zeros], axis=1),
                                       jnp.concatenate([zeros, k2], axis=1)], axis=0)
                p = jnp.where(mask, _dot_nt(lhs, rhs), 0.0 if p is None else p)
            p_scr[c % 2, hh] = p.astype(BF16)

    def carried(c):
        rs = slice(c * C, (c + 1) * C)
        seg = c // chunks_per_seg
        for hh, cs in enumerate(heads):
            state_t = s_scr[seg, hh]
            qs = (q_scr[rs, cs] * jnp.exp2(e_scr[c, 0:C, cs])).astype(BF16)
            ks = (k_scr[rs, cs] * jnp.exp2(e_scr[c, C:2 * C, cs])).astype(BF16)
            o_scr[rs, cs] = _dot_nt(qs, state_t.astype(BF16))
            dec = jnp.exp2(e_scr[c, e_rows - SUBLANES:e_rows - SUBLANES + 1, cs])
            s_scr[seg, hh] = state_t * dec + _dot_tn(proj_scr[rs, 2 * A_WIDTH + hh * A_DIM:
                                                              2 * A_WIDTH + (hh + 1) * A_DIM].astype(BF16), ks)

    def weighted(c):
        rs = slice(c * C, (c + 1) * C)
        for hh, cs in enumerate(heads):
            v = proj_scr[rs, 2 * A_WIDTH + hh * A_DIM:2 * A_WIDTH + (hh + 1) * A_DIM].astype(BF16)
            o_scr[rs, cs] += _dot(p_scr[c % 2, hh], jnp.concatenate([v, v], axis=0))

    scores(0)
    for c in range(n_chunks):
        carried(c)
        if c + 1 < n_chunks:
            scores(c + 1)
        weighted(c)

    for s in range(nseq):
        ext_scr[pl.ds(s * seg_rows, POOL_ROWS), :] = ph_scr[s]
        ext_scr[pl.ds(s * seg_rows + POOL_ROWS, L), :] = proj_scr[pl.ds(s * L, L), 4 * A_WIDTH:AB_IN]
    for s in range(nseq):
        ph_scr[s] = ext_scr[pl.ds(s * seg_rows + L, POOL_ROWS), :]

    def new_rows(a):
        if nseq == 1:
            return a[POOL_ROWS:]
        return a.reshape(nseq, seg_rows, a.shape[-1])[:, POOL_ROWS:, :].reshape(Tt, a.shape[-1])

    pos = pos0 + t * L + (lax.broadcasted_iota(jnp.int32, (Tt, 1), 0) & (L - 1))
    og = og_ref[...]
    for hh in range(A_HEADS):
        c0 = hh * A_DIM
        oh = o_scr[:, c0:c0 + A_DIM]
        gate = proj_scr[:, 3 * A_WIDTH + c0:3 * A_WIDTH + c0 + A_DIM]
        cat_scr[:, c0:c0 + A_DIM] = (_rms(oh, og) * (gate * _sigmoid(gate))).astype(BF16)
    for gi, w in enumerate(POOL_WINDOWS):
        c0 = gi * B_GROUP_DIM
        cur = ext_scr[:, c0:c0 + B_GROUP_DIM]
        acc = cur
        sh = 1
        while sh < w:
            acc = acc + pltpu.roll(acc, sh, 0)
            sh *= 2
        cnt = jnp.minimum(pos + 1, w).astype(F32)
        pooled = new_rows(acc) / cnt - new_rows(cur)
        yb = _dot(pooled.astype(BF16), pw_ref[gi]) * ps_ref[:, c0:c0 + B_GROUP_DIM]
        cat_scr[:, A_WIDTH + c0:A_WIDTH + c0 + B_GROUP_DIM] = yb.astype(BF16)

    xo_ref[...] = x + _dot(cat_scr[...], wout_ref[...])
    for s in range(nseq):
        for hh in range(A_HEADS):
            sout_ref[s, hh] = s_scr[s, hh].T
    pout_ref[...] = ph_scr[...]


def _full_spec(shape):
    zeros = (0,) * len(shape)
    return pl.BlockSpec(shape, lambda g, t: zeros, pipeline_mode=pl.Buffered(1))


def _even_layer(layer_e, n_even, x, s_hgrn, s_pool, gain, lb_logits, w_in, out_gain, pool_w, pool_scale, w_out,
                *, nseq, L, C, pos0):
    G, T, _ = x.shape
    Tt = nseq * L
    has_state = s_hgrn is not None
    body = functools.partial(_even_body, layer_e, n_even, nseq, L, C, pos0, has_state)
    x_spec = pl.BlockSpec((None, Tt, D_MODEL), lambda g, t: (g, t, 0))
    s_spec = pl.BlockSpec((nseq, A_HEADS, A_DIM, A_DIM), lambda g, t: (g, 0, 0, 0))
    p_spec = pl.BlockSpec((nseq, POOL_ROWS, B_WIDTH), lambda g, t: (g, 0, 0))
    m3 = jnp.asarray(_hgrn_sum_matrix(C), BF16)
    weights = (gain, lb_logits, w_in, out_gain, pool_w, pool_scale, w_out, m3)
    w_specs = [_full_spec(w.shape) for w in weights]
    state_in = (s_hgrn, s_pool) if has_state else ()
    state_specs = [s_spec, p_spec] if has_state else []
    return pl.pallas_call(
        body,
        grid=(G, T // Tt),
        in_specs=[x_spec] + state_specs + w_specs,
        out_specs=(x_spec, s_spec, p_spec),
        out_shape=(jax.ShapeDtypeStruct(x.shape, F32),
                   jax.ShapeDtypeStruct((G * nseq, A_HEADS, A_DIM, A_DIM), F32),
                   jax.ShapeDtypeStruct((G * nseq, POOL_ROWS, B_WIDTH), F32)),
        scratch_shapes=[pltpu.VMEM((Tt, AB_IN), F32), pltpu.VMEM((Tt, A_WIDTH), F32),
                        pltpu.VMEM((Tt, A_WIDTH + B_WIDTH), BF16),
                        pltpu.VMEM((nseq * (POOL_ROWS + L), B_WIDTH), F32),
                        pltpu.VMEM((Tt, A_WIDTH), F32), pltpu.VMEM((Tt, A_WIDTH), F32),
                        pltpu.VMEM((Tt // C, m3.shape[0], A_WIDTH), F32),
                        pltpu.VMEM((2, A_HEADS, C, 2 * C), BF16),
                        pltpu.VMEM((nseq, A_HEADS, A_DIM, A_DIM), F32),
                        pltpu.VMEM((nseq, POOL_ROWS, B_WIDTH), F32)],
        compiler_params=pltpu.CompilerParams(dimension_semantics=("arbitrary", "arbitrary"),
                                             vmem_limit_bytes=V7X_VMEM_LIMIT_BYTES),
        name=f"even_mixer_{layer_e}_{'s' if has_state else 'p'}",
    )(x, *state_in, *weights)


def _odd_body(Tt, Lc, want_v, *refs):
    if want_v:
        (x_ref, gain_ref, wuv_ref, lng_ref, lnb_ref, wmix_ref, bias_ref, wout_ref,
         xo_ref, vo_ref, zv_scr, u_scr, vn_scr, m_scr, acc_scr) = refs
    else:
        (x_ref, gain_ref, wuv_ref, lng_ref, lnb_ref, wmix_ref, bias_ref, wout_ref,
         xo_ref, zv_scr, u_scr, vn_scr, m_scr, acc_scr) = refs
    x = x_ref[...]
    h = _rms(x, gain_ref[...]).astype(BF16)
    zv_scr[...] = _dot(h, wuv_ref[:, C_WIDTH:])
    for g in range(C_GROUPS):
        u_scr[g] = _dot(h, wuv_ref[:, g * C_GROUP_DIM:(g + 1) * C_GROUP_DIM])
    zv = _gelu(zv_scr[...])
    mu = jnp.mean(zv, axis=-1, keepdims=True)
    zc = zv - mu
    vn = zc * lax.rsqrt(jnp.mean(zc * zc, axis=-1, keepdims=True) + EPS) * lng_ref[...] + lnb_ref[...]
    if want_v:
        vo_ref[...] = vn
    vn_scr[...] = vn.astype(BF16)

    def gate_rows(g):
        c0 = g * C_GROUP_DIM
        bias = bias_ref[:, g:g + 1]
        for b in range(Tt // Lc):
            m_scr[g % 2, pl.ds(b * Lc, Lc), :] = (
                _dot(wmix_ref[g], vn_scr[pl.ds(b * Lc, Lc), c0:c0 + C_GROUP_DIM]) + bias)

    gate_rows(0)
    for g in range(C_GROUPS):
        if g + 1 < C_GROUPS:
            gate_rows(g + 1)
        y = (_gelu(u_scr[g]) * m_scr[g % 2]).astype(BF16)
        part = _dot(y, wout_ref[g * C_GROUP_DIM:(g + 1) * C_GROUP_DIM, :])
        if g == 0:
            acc_scr[...] = part
        else:
            acc_scr[...] += part
    xo_ref[...] = x + acc_scr[...]


def _odd_layer(x, gain, w_uv, ln_gain, ln_bias, w_mix, bias, w_out, *, Tt, want_v):
    G, T, _ = x.shape
    Lc = w_mix.shape[-1]
    body = functools.partial(_odd_body, Tt, Lc, want_v)
    x_spec = pl.BlockSpec((None, Tt, D_MODEL), lambda g, t: (g, t, 0))
    weights = (gain, w_uv, ln_gain, ln_bias, w_mix, bias, w_out)
    out_specs = [x_spec]
    out_shape = [jax.ShapeDtypeStruct(x.shape, F32)]
    if want_v:
        out_specs.append(pl.BlockSpec((None, Tt, C_WIDTH), lambda g, t: (g, t, 0)))
        out_shape.append(jax.ShapeDtypeStruct((G, T, C_WIDTH), F32))
    return pl.pallas_call(
        body,
        grid=(G, T // Tt),
        in_specs=[x_spec] + [_full_spec(w.shape) for w in weights],
        out_specs=tuple(out_specs),
        out_shape=tuple(out_shape),
        scratch_shapes=[pltpu.VMEM((Tt, C_WIDTH), F32), pltpu.VMEM((C_GROUPS, Tt, C_GROUP_DIM), F32),
                        pltpu.VMEM((Tt, C_WIDTH), BF16), pltpu.VMEM((2, Tt, C_GROUP_DIM), F32),
                        pltpu.VMEM((Tt, D_MODEL), F32)],
        compiler_params=pltpu.CompilerParams(dimension_semantics=("arbitrary", "arbitrary"),
                                             vmem_limit_bytes=V7X_VMEM_LIMIT_BYTES),
        name=f"odd_mixer_{'s' if want_v else 'p'}",
    )(x, *weights)


def _ffn_body(nseq, L, has_state, final_norm, *refs):
    refs = list(refs)
    x_ref = refs.pop(0)
    hin_ref = refs.pop(0) if has_state else None
    gain_ref, win_ref, cw_ref, cb_ref, wd_ref = refs[:5]
    refs = refs[5:]
    gf_ref = refs.pop(0) if final_norm else None
    xo_ref, hout_ref, acc_scr, hist_scr, ag_scr = refs
    t = pl.program_id(1)
    Tt = nseq * L

    @pl.when(t == 0)
    def _():
        if has_state:
            hist_scr[0] = hin_ref[:, 0, :]
            hist_scr[1] = hin_ref[:, 1, :]
        else:
            hist_scr[...] = jnp.zeros_like(hist_scr)

    x = x_ref[...]
    h = _rms(x, gain_ref[...]).astype(BF16)
    rowm = lax.broadcasted_iota(jnp.int32, (Tt, 1), 0) & (L - 1)

    def expand(rows):
        if nseq == 1:
            return rows
        return jnp.broadcast_to(rows[:, None, :], (nseq, L, rows.shape[-1])).reshape(Tt, rows.shape[-1])

    def up(j):
        for half, c0 in enumerate((j * FF_CHUNK, D_FF + j * FF_CHUNK)):
            ag_scr[j % 2, :, :, half * FF_CHUNK:(half + 1) * FF_CHUNK] = (
                _dot(h, win_ref[:, c0:c0 + FF_CHUNK]).reshape(nseq, L, FF_CHUNK))

    up(0)
    for j in range(N_FF_CHUNKS):
        cols = slice(j * FF_CHUNK, (j + 1) * FF_CHUNK)
        slot = j % 2
        if j + 1 < N_FF_CHUNKS:
            up(j + 1)
        a = ag_scr[slot, :, :, :FF_CHUNK].reshape(Tt, FF_CHUNK)
        gate = ag_scr[slot, :, :, FF_CHUNK:].reshape(Tt, FF_CHUNK)
        h2 = expand(hist_scr[0, :, cols])
        h1 = expand(hist_scr[1, :, cols])
        a1 = jnp.where(rowm >= 1, pltpu.roll(a, 1, 0), h1)
        a2 = jnp.where(rowm >= 2, pltpu.roll(a, 2, 0), jnp.where(rowm == 1, h1, h2))
        conv = (cb_ref[:, cols] + a2 * cw_ref[pl.ds(0, 1), cols] + a1 * cw_ref[pl.ds(1, 1), cols]
                + a * cw_ref[pl.ds(2, 1), cols])
        y = (_gelu(conv) * gate).astype(BF16)
        part = _dot(y, wd_ref[cols, :])
        if j == 0:
            acc_scr[...] = part
        else:
            acc_scr[...] += part
        hist_scr[0, :, cols] = ag_scr[slot, :, L - 2, :FF_CHUNK]
        hist_scr[1, :, cols] = ag_scr[slot, :, L - 1, :FF_CHUNK]
    xn = x + acc_scr[...]
    if final_norm:
        xn = _rms(xn, gf_ref[...])
    xo_ref[...] = xn
    hout_ref[:, 0, :] = hist_scr[0]
    hout_ref[:, 1, :] = hist_scr[1]


def _ffn_layer(x, s_conv, gain, w_in, conv_w, conv_b, w_d, final_gain, *, nseq, L):
    G, T, _ = x.shape
    Tt = nseq * L
    has_state = s_conv is not None
    final_norm = final_gain is not None
    body = functools.partial(_ffn_body, nseq, L, has_state, final_norm)
    x_spec = pl.BlockSpec((None, Tt, D_MODEL), lambda g, t: (g, t, 0))
    h_spec = pl.BlockSpec((nseq, CONV_W - 1, D_FF), lambda g, t: (g, 0, 0))
    weights = (gain, w_in, conv_w, conv_b, w_d) + ((final_gain,) if final_norm else ())
    state_in = (s_conv,) if has_state else ()
    return pl.pallas_call(
        body,
        grid=(G, T // Tt),
        in_specs=[x_spec] + ([h_spec] if has_state else []) + [_full_spec(w.shape) for w in weights],
        out_specs=(x_spec, h_spec),
        out_shape=(jax.ShapeDtypeStruct(x.shape, F32),
                   jax.ShapeDtypeStruct((G * nseq, CONV_W - 1, D_FF), F32)),
        scratch_shapes=[pltpu.VMEM((Tt, D_MODEL), F32), pltpu.VMEM((CONV_W - 1, nseq, D_FF), F32),
                        pltpu.VMEM((2, nseq, L, 2 * FF_CHUNK), F32)],
        compiler_params=pltpu.CompilerParams(dimension_semantics=("arbitrary", "arbitrary"),
                                             vmem_limit_bytes=V7X_VMEM_LIMIT_BYTES),
        name=f"conv_ffn_{'s' if has_state else 'p'}{'_final' if final_norm else ''}",
    )(x, *state_in, *weights)


def _trunk(x, s_hgrn, s_pool, s_conv, pos0, w, *, nseq, L, want_v):
    depth = w["norm_mix"].shape[0]
    n_even = w["lb_logits"].shape[0]
    Tt = nseq * L
    C = min(HGRN_CHUNK, L)
    new_hgrn, new_pool, new_conv, new_cv = [], [], [], []
    for layer in range(depth):
        if layer % 2 == 0:
            e = layer // 2
            x, sh, sp = _even_layer(
                e, n_even, x, None if s_hgrn is None else s_hgrn[e], None if s_pool is None else s_pool[e],
                w["norm_mix"][layer][None], w["lb_logits"], w["w_in_ab"][e], w["hgrn_out_gain"][e][None],
                w["pool_w"][e], w["pool_scale"][e][None], w["w_out_ab"][e], nseq=nseq, L=L, C=C, pos0=pos0)
            new_hgrn.append(sh)
            new_pool.append(sp)
        else:
            o = layer // 2
            out = _odd_layer(x, w["norm_mix"][layer][None], w["w_uv"][o], w["c_ln_gain"][o][None],
                             w["c_ln_bias"][o][None], w["w_mix"][o], w["c_bias"][o], w["w_out_c"][o],
                             Tt=Tt, want_v=want_v)
            x = out[0]
            if want_v:
                new_cv.append(out[1])
        final_gain = w["norm_final"][None] if layer == depth - 1 else None
        x, sc = _ffn_layer(x, None if s_conv is None else s_conv[layer], w["norm_ffn"][layer][None],
                           w["w_in_ffn"][layer], w["conv_w"][layer], w["conv_b"][layer],
                           w["w_d"][layer], final_gain, nseq=nseq, L=L)
        new_conv.append(sc)
    return x, jnp.stack(new_hgrn), jnp.stack(new_pool), jnp.stack(new_conv), new_cv


def kernel(x_prompt, x_sample, state_hgrn, state_pool, state_ffn_conv, norm_mix, norm_ffn, norm_final, hgrn_lb_logits, w_in_ab, hgrn_out_gain, pool_w, pool_scale, w_out_ab, w_uv_c, c_ln_gain, c_ln_bias, c_ws, c_bs, w_out_c, w_in_ffn, ffn_conv_w, ffn_conv_b, w_out_ffn):
    depth = norm_mix.shape[0]
    n_odd = w_uv_c.shape[0]
    bp, seq, _ = x_prompt.shape
    bs, dseq, _ = x_sample.shape

    common = dict(
        norm_mix=norm_mix, norm_ffn=norm_ffn, norm_final=norm_final, lb_logits=hgrn_lb_logits,
        w_in_ab=w_in_ab.astype(BF16), hgrn_out_gain=hgrn_out_gain, pool_w=pool_w.astype(BF16),
        pool_scale=pool_scale, w_out_ab=w_out_ab.astype(BF16),
        w_uv=w_uv_c.astype(BF16), c_ln_gain=c_ln_gain, c_ln_bias=c_ln_bias, w_out_c=w_out_c.astype(BF16),
        w_in_ffn=w_in_ffn.astype(BF16), conv_w=ffn_conv_w, conv_b=ffn_conv_b[:, None, :],
        w_d=w_out_ffn.astype(BF16),
    )

    def gating(length, copies):
        idx = jnp.arange(length)
        mask = (idx[None, :] // C_CAUSAL) <= (idx[:, None] // C_CAUSAL)
        wm = jnp.where(mask[None, None], c_ws[:, :, :length, :length], 0.0)
        if copies > 1:
            eye = jnp.eye(copies, dtype=wm.dtype)
            wm = jnp.einsum("ab,ogts->ogatbs", eye, wm).reshape(n_odd, C_GROUPS, copies * length, copies * length)
        bias = jnp.tile(jnp.swapaxes(c_bs[:, :, :length], 1, 2), (1, copies, 1))
        return wm.astype(BF16), bias

    wm_p, bias_p = gating(min(C_BLOCK, seq), 1)
    Lp = min(PROMPT_TILE, seq)
    y_p, hg_p, pool_p, conv_p, _ = _trunk(x_prompt, None, None, None, 0, dict(common, w_mix=wm_p, c_bias=bias_p),
                                          nseq=1, L=Lp, want_v=False)

    wm_s, bias_s = gating(min(C_BLOCK, dseq), bs)
    pool_hist = jnp.pad(state_pool, ((0, 0), (0, 0), (POOL_ROWS - state_pool.shape[2], 0), (0, 0)))
    y_s, hg_s, pool_s, conv_s, cv_s = _trunk(
        x_sample.reshape(1, bs * dseq, D_MODEL), state_hgrn, pool_hist, state_ffn_conv, PAST_LEN,
        dict(common, w_mix=wm_s, c_bias=bias_s), nseq=bs, L=dseq, want_v=True)
    n_hist = state_pool.shape[2]
    cv_s = jnp.stack(cv_s).reshape(n_odd, bs, dseq, C_WIDTH)
    return (y_p, y_s.reshape(bs, dseq, D_MODEL), hg_p, hg_s, pool_p[:, :, POOL_ROWS - n_hist:],
            pool_s[:, :, POOL_ROWS - n_hist:], conv_p, conv_s, cv_s)
```

```python
import functools

import numpy as np
import jax
import jax.numpy as jnp
from jax import lax
from jax.experimental import pallas as pl
from jax.experimental.pallas import tpu as pltpu

F32 = jnp.float32
BF16 = jnp.bfloat16

D_MODEL = 1024
EPS = 1e-6
PAST_LEN = 4096
SUBLANES = 8
A_HEADS = 4
A_DIM = 128
A_WIDTH = A_HEADS * A_DIM
HGRN_CHUNK = 64
LOG2_E = 1.4426950408889634
POOL_WINDOWS = (2, 4, 8, 16)
B_GROUP_DIM = 128
B_WIDTH = len(POOL_WINDOWS) * B_GROUP_DIM
POOL_ROWS = 16
AB_IN = 4 * A_WIDTH + B_WIDTH
C_BLOCK = 128
C_CAUSAL = 64
C_GROUPS = 8
C_WIDTH = 2 * D_MODEL
C_GROUP_DIM = C_WIDTH // C_GROUPS
D_FF = 2816
FF_CHUNK = 256
N_FF_CHUNKS = D_FF // FF_CHUNK
CONV_W = 3

PROMPT_TILE = 512
V7X_VMEM_LIMIT_BYTES = 56 * 1024 * 1024


def _dot(a, b):
    return jnp.dot(a, b, preferred_element_type=F32)


def _dot_nt(a, b):
    return lax.dot_general(a, b, (((1,), (1,)), ((), ())), preferred_element_type=F32)


def _dot_tn(a, b):
    return lax.dot_general(a, b, (((0,), (0,)), ((), ())), preferred_element_type=F32)


def _rms(x, gain):
    return x * lax.rsqrt(jnp.mean(x * x, axis=-1, keepdims=True) + EPS) * gain


def _sigmoid(x):
    return 1.0 / (1.0 + jnp.exp(-x))


def _gelu(x):
    return 0.5 * x * (1.0 + lax.erf(x * 0.7071067811865476))


def _hgrn_levels(C):
    levels, size = [], 2
    while size <= C:
        levels.append(size)
        size *= 2
    return levels


def _hgrn_sum_matrix(C):
    t = np.arange(C)[:, None]
    s = np.arange(C)[None, :]
    blocks = [s <= t, s > t]
    for size in _hgrn_levels(C):
        half = size // 2
        same = (t // size) == (s // size)
        blocks.append((same & (t % size >= half) & (s % size >= half) & (s <= t))
                      | (same & (t % size < half) & (s % size < half) & (s > t)))
    blocks.append(np.ones((SUBLANES, C), bool))
    m = np.concatenate(blocks, axis=0).astype(np.float32)
    return np.concatenate([m, m, m], axis=1)


def _even_body(layer_e, n_even, nseq, L, C, pos0, has_state, *refs):
    if has_state:
        (x_ref, sin_ref, pin_ref, gain_ref, lbl_ref, win_ref, og_ref, pw_ref, ps_ref, wout_ref, m3_ref,
         xo_ref, sout_ref, pout_ref, proj_scr, o_scr, cat_scr, ext_scr, q_scr, k_scr, e_scr, p_scr, s_scr, ph_scr) = refs
    else:
        (x_ref, gain_ref, lbl_ref, win_ref, og_ref, pw_ref, ps_ref, wout_ref, m3_ref,
         xo_ref, sout_ref, pout_ref, proj_scr, o_scr, cat_scr, ext_scr, q_scr, k_scr, e_scr, p_scr, s_scr, ph_scr) = refs
    t = pl.program_id(1)
    Tt = nseq * L
    seg_rows = POOL_ROWS + L
    levels = _hgrn_levels(C)
    e_rows = e_scr.shape[1]

    @pl.when(t == 0)
    def _():
        if has_state:
            for s in range(nseq):
                for hh in range(A_HEADS):
                    s_scr[s, hh] = sin_ref[s, hh].T
            ph_scr[...] = pin_ref[...]
        else:
            s_scr[...] = jnp.zeros_like(s_scr)
            ph_scr[...] = jnp.zeros_like(ph_scr)

    x = x_ref[...]
    h = _rms(x, gain_ref[...]).astype(BF16)
    proj_scr[...] = _dot(h, win_ref[...])

    lrows = [lbl_ref[pl.ds(i, 1), :] for i in range(n_even)]
    lmax = functools.reduce(jnp.maximum, lrows)
    lexp = [jnp.exp(r - lmax) for r in lrows]
    lden = functools.reduce(lambda a, b: a + b, lexp)
    lb = jnp.zeros_like(lden)
    for i in range(1, layer_e + 1):
        lb = lb + lexp[i] / lden
    log_lb = jnp.log(lb)
    log1m_lb = jnp.log1p(-lb)
    one_m_lb = 1.0 - lb

    level_ids = [0] + levels
    if len(level_ids) % 2:
        level_ids.append(-1)
    level_pairs = [(level_ids[i], level_ids[i + 1]) for i in range(0, len(level_ids), 2)]
    qi = lax.broadcasted_iota(jnp.int32, (C, 2 * C), 0)
    kj = lax.broadcasted_iota(jnp.int32, (C, 2 * C), 1)
    ki = kj & (C - 1)

    def owns(size):
        if size == 0:
            return qi == ki
        return ((qi // size) == (ki // size)) & ((qi & (size - 1)) >= size // 2) & ((ki & (size - 1)) < size // 2)

    pair_owned = []
    for first, second in level_pairs:
        m = (kj < C) & owns(first)
        if second >= 0:
            m = m | ((kj >= C) & owns(second))
        pair_owned.append(m)
    chunks_per_seg = L // C

    n_chunks = Tt // C
    heads = [slice(hh * A_DIM, (hh + 1) * A_DIM) for hh in range(A_HEADS)]

    qr = proj_scr[:, 0:A_WIDTH]
    fz = proj_scr[:, A_WIDTH:2 * A_WIDTH]
    q_scr[...] = qr * _sigmoid(qr) * (A_DIM ** -0.5)
    log_sig = jnp.minimum(fz, 0.0) - jnp.log(1.0 + jnp.exp(-jnp.abs(fz)))
    y = log1m_lb + log_sig
    lf = jnp.maximum(log_lb, y) + jnp.log(1.0 + jnp.exp(-jnp.abs(log_lb - y)))
    k_scr[...] = one_m_lb * _sigmoid(-fz)
    lf2 = lf * LOG2_E
    hi = lf2.astype(BF16)
    r1 = lf2 - hi.astype(F32)
    mid = r1.astype(BF16)
    lo = (r1 - mid.astype(F32)).astype(BF16)
    for c in range(n_chunks):
        rs = slice(c * C, (c + 1) * C)
        e_scr[c] = _dot(m3_ref[...], jnp.concatenate([hi[rs], mid[rs], lo[rs]], axis=0))

    def scores(c):
        rs = slice(c * C, (c + 1) * C)
        zeros = jnp.zeros((C, A_DIM), BF16)
        for hh, cs in enumerate(heads):
            q_b, k_b = q_scr[rs, cs].astype(BF16), k_scr[rs, cs].astype(BF16)

            def sides(size):
                if size <= 0:
                    return q_b, k_b
                r0 = (2 + levels.index(size)) * C
                decay = jnp.exp2(e_scr[c, r0:r0 + C, cs]).astype(BF16)
                return q_b * decay, k_b * decay

            p = None
            for (first, second), mask in zip(level_pairs, pair_owned):
                (q1, k1), (q2, k2) = sides(first), sides(second)
                lhs = jnp.concatenate([q1, q2], axis=1)
                rhs = jnp.concatenate([jnp.concatenate([k1, zeros], axis=1),
                                       jnp.concatenate([zeros, k2], axis=1)], axis=0)
                p = jnp.where(mask, _dot_nt(lhs, rhs), 0.0 if p is None else p)
            p_scr[c % 2, hh] = p.astype(BF16)

    def carried(c):
        rs = slice(c * C, (c + 1) * C)
        seg = c // chunks_per_seg
        for hh, cs in enumerate(heads):
            state_t = s_scr[seg, hh]
            qs = (q_scr[rs, cs] * jnp.exp2(e_scr[c, 0:C, cs])).astype(BF16)
            ks = (k_scr[rs, cs] * jnp.exp2(e_scr[c, C:2 * C, cs])).astype(BF16)
            o_scr[rs, cs] = _dot_nt(qs, state_t.astype(BF16))
            dec = jnp.exp2(e_scr[c, e_rows - SUBLANES:e_rows - SUBLANES + 1, cs])
            s_scr[seg, hh] = state_t * dec + _dot_tn(proj_scr[rs, 2 * A_WIDTH + hh * A_DIM:
                                                              2 * A_WIDTH + (hh + 1) * A_DIM].astype(BF16), ks)

    def weighted(c):
        rs = slice(c * C, (c + 1) * C)
        for hh, cs in enumerate(heads):
            v = proj_scr[rs, 2 * A_WIDTH + hh * A_DIM:2 * A_WIDTH + (hh + 1) * A_DIM].astype(BF16)
            o_scr[rs, cs] += _dot(p_scr[c % 2, hh], jnp.concatenate([v, v], axis=0))

    scores(0)
    for c in range(n_chunks):
        carried(c)
        if c + 1 < n_chunks:
            scores(c + 1)
        weighted(c)

    for s in range(nseq):
        ext_scr[pl.ds(s * seg_rows, POOL_ROWS), :] = ph_scr[s]
        ext_scr[pl.ds(s * seg_rows + POOL_ROWS, L), :] = proj_scr[pl.ds(s * L, L), 4 * A_WIDTH:AB_IN]
    for s in range(nseq):
        ph_scr[s] = ext_scr[pl.ds(s * seg_rows + L, POOL_ROWS), :]

    def new_rows(a):
        if nseq == 1:
            return a[POOL_ROWS:]
        return a.reshape(nseq, seg_rows, a.shape[-1])[:, POOL_ROWS:, :].reshape(Tt, a.shape[-1])

    pos = pos0 + t * L + (lax.broadcasted_iota(jnp.int32, (Tt, 1), 0) & (L - 1))
    og = og_ref[...]
    for hh in range(A_HEADS):
        c0 = hh * A_DIM
        oh = o_scr[:, c0:c0 + A_DIM]
        gate = proj_scr[:, 3 * A_WIDTH + c0:3 * A_WIDTH + c0 + A_DIM]
        cat_scr[:, c0:c0 + A_DIM] = (_rms(oh, og) * (gate * _sigmoid(gate))).astype(BF16)
    for gi, w in enumerate(POOL_WINDOWS):
        c0 = gi * B_GROUP_DIM
        cur = ext_scr[:, c0:c0 + B_GROUP_DIM]
        acc = cur
        sh = 1
        while sh < w:
            acc = acc + pltpu.roll(acc, sh, 0)
            sh *= 2
        cnt = jnp.minimum(pos + 1, w).astype(F32)
        pooled = new_rows(acc) / cnt - new_rows(cur)
        yb = _dot(pooled.astype(BF16), pw_ref[gi]) * ps_ref[:, c0:c0 + B_GROUP_DIM]
        cat_scr[:, A_WIDTH + c0:A_WIDTH + c0 + B_GROUP_DIM] = yb.astype(BF16)

    xo_ref[...] = x + _dot(cat_scr[...], wout_ref[...])
    for s in range(nseq):
        for hh in range(A_HEADS):
            sout_ref[s, hh] = s_scr[s, hh].T
    pout_ref[...] = ph_scr[...]


def _full_spec(shape):
    zeros = (0,) * len(shape)
    return pl.BlockSpec(shape, lambda g, t: zeros, pipeline_mode=pl.Buffered(1))


def _even_layer(layer_e, n_even, x, s_hgrn, s_pool, gain, lb_logits, w_in, out_gain, pool_w, pool_scale, w_out,
                *, nseq, L, C, pos0):
    G, T, _ = x.shape
    Tt = nseq * L
    has_state = s_hgrn is not None
    body = functools.partial(_even_body, layer_e, n_even, nseq, L, C, pos0, has_state)
    x_spec = pl.BlockSpec((None, Tt, D_MODEL), lambda g, t: (g, t, 0))
    s_spec = pl.BlockSpec((nseq, A_HEADS, A_DIM, A_DIM), lambda g, t: (g, 0, 0, 0))
    p_spec = pl.BlockSpec((nseq, POOL_ROWS, B_WIDTH), lambda g, t: (g, 0, 0))
    m3 = jnp.asarray(_hgrn_sum_matrix(C), BF16)
    weights = (gain, lb_logits, w_in, out_gain, pool_w, pool_scale, w_out, m3)
    w_specs = [_full_spec(w.shape) for w in weights]
    state_in = (s_hgrn, s_pool) if has_state else ()
    state_specs = [s_spec, p_spec] if has_state else []
    return pl.pallas_call(
        body,
        grid=(G, T // Tt),
        in_specs=[x_spec] + state_specs + w_specs,
        out_specs=(x_spec, s_spec, p_spec),
        out_shape=(jax.ShapeDtypeStruct(x.shape, F32),
                   jax.ShapeDtypeStruct((G * nseq, A_HEADS, A_DIM, A_DIM), F32),
                   jax.ShapeDtypeStruct((G * nseq, POOL_ROWS, B_WIDTH), F32)),
        scratch_shapes=[pltpu.VMEM((Tt, AB_IN), F32), pltpu.VMEM((Tt, A_WIDTH), F32),
                        pltpu.VMEM((Tt, A_WIDTH + B_WIDTH), BF16),
                        pltpu.VMEM((nseq * (POOL_ROWS + L), B_WIDTH), F32),
                        pltpu.VMEM((Tt, A_WIDTH), F32), pltpu.VMEM((Tt, A_WIDTH), F32),
                        pltpu.VMEM((Tt // C, m3.shape[0], A_WIDTH), F32),
                        pltpu.VMEM((2, A_HEADS, C, 2 * C), BF16),
                        pltpu.VMEM((nseq, A_HEADS, A_DIM, A_DIM), F32),
                        pltpu.VMEM((nseq, POOL_ROWS, B_WIDTH), F32)],
        compiler_params=pltpu.CompilerParams(dimension_semantics=("arbitrary", "arbitrary"),
                                             vmem_limit_bytes=V7X_VMEM_LIMIT_BYTES),
        name=f"even_mixer_{layer_e}_{'s' if has_state else 'p'}",
    )(x, *state_in, *weights)


def _odd_body(Tt, Lc, want_v, *refs):
    if want_v:
        (x_ref, gain_ref, wuv_ref, lng_ref, lnb_ref, wmix_ref, bias_ref, wout_ref,
         xo_ref, vo_ref, zv_scr, u_scr, vn_scr, m_scr, acc_scr) = refs
    else:
        (x_ref, gain_ref, wuv_ref, lng_ref, lnb_ref, wmix_ref, bias_ref, wout_ref,
         xo_ref, zv_scr, u_scr, vn_scr, m_scr, acc_scr) = refs
    x = x_ref[...]
    h = _rms(x, gain_ref[...]).astype(BF16)
    zv_scr[...] = _dot(h, wuv_ref[:, C_WIDTH:])
    for g in range(C_GROUPS):
        u_scr[g] = _dot(h, wuv_ref[:, g * C_GROUP_DIM:(g + 1) * C_GROUP_DIM])
    zv = _gelu(zv_scr[...])
    mu = jnp.mean(zv, axis=-1, keepdims=True)
    zc = zv - mu
    vn = zc * lax.rsqrt(jnp.mean(zc * zc, axis=-1, keepdims=True) + EPS) * lng_ref[...] + lnb_ref[...]
    if want_v:
        vo_ref[...] = vn
    vn_scr[...] = vn.astype(BF16)

    def gate_rows(g):
        c0 = g * C_GROUP_DIM
        bias = bias_ref[:, g:g + 1]
        for b in range(Tt // Lc):
            m_scr[g % 2, pl.ds(b * Lc, Lc), :] = (
                _dot(wmix_ref[g], vn_scr[pl.ds(b * Lc, Lc), c0:c0 + C_GROUP_DIM]) + bias)

    gate_rows(0)
    for g in range(C_GROUPS):
        if g + 1 < C_GROUPS:
            gate_rows(g + 1)
        y = (_gelu(u_scr[g]) * m_scr[g % 2]).astype(BF16)
        part = _dot(y, wout_ref[g * C_GROUP_DIM:(g + 1) * C_GROUP_DIM, :])
        if g == 0:
            acc_scr[...] = part
        else:
            acc_scr[...] += part
    xo_ref[...] = x + acc_scr[...]


def _odd_layer(x, gain, w_uv, ln_gain, ln_bias, w_mix, bias, w_out, *, Tt, want_v):
    G, T, _ = x.shape
    Lc = w_mix.shape[-1]
    body = functools.partial(_odd_body, Tt, Lc, want_v)
    x_spec = pl.BlockSpec((None, Tt, D_MODEL), lambda g, t: (g, t, 0))
    weights = (gain, w_uv, ln_gain, ln_bias, w_mix, bias, w_out)
    out_specs = [x_spec]
    out_shape = [jax.ShapeDtypeStruct(x.shape, F32)]
    if want_v:
        out_specs.append(pl.BlockSpec((None, Tt, C_WIDTH), lambda g, t: (g, t, 0)))
        out_shape.append(jax.ShapeDtypeStruct((G, T, C_WIDTH), F32))
    return pl.pallas_call(
        body,
        grid=(G, T // Tt),
        in_specs=[x_spec] + [_full_spec(w.shape) for w in weights],
        out_specs=tuple(out_specs),
        out_shape=tuple(out_shape),
        scratch_shapes=[pltpu.VMEM((Tt, C_WIDTH), F32), pltpu.VMEM((C_GROUPS, Tt, C_GROUP_DIM), F32),
                        pltpu.VMEM((Tt, C_WIDTH), BF16), pltpu.VMEM((2, Tt, C_GROUP_DIM), F32),
                        pltpu.VMEM((Tt, D_MODEL), F32)],
        compiler_params=pltpu.CompilerParams(dimension_semantics=("arbitrary", "arbitrary"),
                                             vmem_limit_bytes=V7X_VMEM_LIMIT_BYTES),
        name=f"odd_mixer_{'s' if want_v else 'p'}",
    )(x, *weights)


def _ffn_body(nseq, L, has_state, final_norm, *refs):
    refs = list(refs)
    x_ref = refs.pop(0)
    hin_ref = refs.pop(0) if has_state else None
    gain_ref, win_ref, cw_ref, cb_ref, wd_ref = refs[:5]
    refs = refs[5:]
    gf_ref = refs.pop(0) if final_norm else None
    xo_ref, hout_ref, acc_scr, hist_scr, ag_scr, y_scr = refs
    t = pl.program_id(1)
    Tt = nseq * L

    @pl.when(t == 0)
    def _():
        if has_state:
            hist_scr[0] = hin_ref[:, 0, :]
            hist_scr[1] = hin_ref[:, 1, :]
        else:
            hist_scr[...] = jnp.zeros_like(hist_scr)

    x = x_ref[...]
    h = _rms(x, gain_ref[...]).astype(BF16)
    rowm = lax.broadcasted_iota(jnp.int32, (Tt, 1), 0) & (L - 1)

    def expand(rows):
        if nseq == 1:
            return rows
        return jnp.broadcast_to(rows[:, None, :], (nseq, L, rows.shape[-1])).reshape(Tt, rows.shape[-1])

    def up(j):
        for half, c0 in enumerate((j * FF_CHUNK, D_FF + j * FF_CHUNK)):
            ag_scr[j % 2, :, :, half * FF_CHUNK:(half + 1) * FF_CHUNK] = (
                _dot(h, win_ref[:, c0:c0 + FF_CHUNK]).reshape(nseq, L, FF_CHUNK))

    def down(j):
        part = _dot(y_scr[j % 2], wd_ref[j * FF_CHUNK:(j + 1) * FF_CHUNK, :])
        if j == 0:
            acc_scr[...] = part
        else:
            acc_scr[...] += part

    up(0)
    for j in range(N_FF_CHUNKS):
        cols = slice(j * FF_CHUNK, (j + 1) * FF_CHUNK)
        slot = j % 2
        if j + 1 < N_FF_CHUNKS:
            up(j + 1)
        a = ag_scr[slot, :, :, :FF_CHUNK].reshape(Tt, FF_CHUNK)
        gate = ag_scr[slot, :, :, FF_CHUNK:].reshape(Tt, FF_CHUNK)
        h2 = expand(hist_scr[0, :, cols])
        h1 = expand(hist_scr[1, :, cols])
        a1 = jnp.where(rowm >= 1, pltpu.roll(a, 1, 0), h1)
        a2 = jnp.where(rowm >= 2, pltpu.roll(a, 2, 0), jnp.where(rowm == 1, h1, h2))
        conv = (cb_ref[:, cols] + a2 * cw_ref[pl.ds(0, 1), cols] + a1 * cw_ref[pl.ds(1, 1), cols]
                + a * cw_ref[pl.ds(2, 1), cols])
        y_scr[slot] = (_gelu(conv) * gate).astype(BF16)
        hist_scr[0, :, cols] = ag_scr[slot, :, L - 2, :FF_CHUNK]
        hist_scr[1, :, cols] = ag_scr[slot, :, L - 1, :FF_CHUNK]
        if j > 0:
            down(j - 1)
    down(N_FF_CHUNKS - 1)
    xn = x + acc_scr[...]
    if final_norm:
        xn = _rms(xn, gf_ref[...])
    xo_ref[...] = xn
    hout_ref[:, 0, :] = hist_scr[0]
    hout_ref[:, 1, :] = hist_scr[1]


def _ffn_layer(x, s_conv, gain, w_in, conv_w, conv_b, w_d, final_gain, *, nseq, L):
    G, T, _ = x.shape
    Tt = nseq * L
    has_state = s_conv is not None
    final_norm = final_gain is not None
    body = functools.partial(_ffn_body, nseq, L, has_state, final_norm)
    x_spec = pl.BlockSpec((None, Tt, D_MODEL), lambda g, t: (g, t, 0))
    h_spec = pl.BlockSpec((nseq, CONV_W - 1, D_FF), lambda g, t: (g, 0, 0))
    weights = (gain, w_in, conv_w, conv_b, w_d) + ((final_gain,) if final_norm else ())
    state_in = (s_conv,) if has_state else ()
    return pl.pallas_call(
        body,
        grid=(G, T // Tt),
        in_specs=[x_spec] + ([h_spec] if has_state else []) + [_full_spec(w.shape) for w in weights],
        out_specs=(x_spec, h_spec),
        out_shape=(jax.ShapeDtypeStruct(x.shape, F32),
                   jax.ShapeDtypeStruct((G * nseq, CONV_W - 1, D_FF), F32)),
        scratch_shapes=[pltpu.VMEM((Tt, D_MODEL), F32), pltpu.VMEM((CONV_W - 1, nseq, D_FF), F32),
                        pltpu.VMEM((2, nseq, L, 2 * FF_CHUNK), F32),
                        pltpu.VMEM((2, Tt, FF_CHUNK), BF16)],
        compiler_params=pltpu.CompilerParams(dimension_semantics=("arbitrary", "arbitrary"),
                                             vmem_limit_bytes=V7X_VMEM_LIMIT_BYTES),
        name=f"conv_ffn_{'s' if has_state else 'p'}{'_final' if final_norm else ''}",
    )(x, *state_in, *weights)


def _trunk(x, s_hgrn, s_pool, s_conv, pos0, w, *, nseq, L, want_v):
    depth = w["norm_mix"].shape[0]
    n_even = w["lb_logits"].shape[0]
    Tt = nseq * L
    C = min(HGRN_CHUNK, L)
    new_hgrn, new_pool, new_conv, new_cv = [], [], [], []
    for layer in range(depth):
        if layer % 2 == 0:
            e = layer // 2
            x, sh, sp = _even_layer(
                e, n_even, x, None if s_hgrn is None else s_hgrn[e], None if s_pool is None else s_pool[e],
                w["norm_mix"][layer][None], w["lb_logits"], w["w_in_ab"][e], w["hgrn_out_gain"][e][None],
                w["pool_w"][e], w["pool_scale"][e][None], w["w_out_ab"][e], nseq=nseq, L=L, C=C, pos0=pos0)
            new_hgrn.append(sh)
            new_pool.append(sp)
        else:
            o = layer // 2
            out = _odd_layer(x, w["norm_mix"][layer][None], w["w_uv"][o], w["c_ln_gain"][o][None],
                             w["c_ln_bias"][o][None], w["w_mix"][o], w["c_bias"][o], w["w_out_c"][o],
                             Tt=Tt, want_v=want_v)
            x = out[0]
            if want_v:
                new_cv.append(out[1])
        final_gain = w["norm_final"][None] if layer == depth - 1 else None
        x, sc = _ffn_layer(x, None if s_conv is None else s_conv[layer], w["norm_ffn"][layer][None],
                           w["w_in_ffn"][layer], w["conv_w"][layer], w["conv_b"][layer],
                           w["w_d"][layer], final_gain, nseq=nseq, L=L)
        new_conv.append(sc)
    return x, jnp.stack(new_hgrn), jnp.stack(new_pool), jnp.stack(new_conv), new_cv


def kernel(x_prompt, x_sample, state_hgrn, state_pool, state_ffn_conv, norm_mix, norm_ffn, norm_final, hgrn_lb_logits, w_in_ab, hgrn_out_gain, pool_w, pool_scale, w_out_ab, w_uv_c, c_ln_gain, c_ln_bias, c_ws, c_bs, w_out_c, w_in_ffn, ffn_conv_w, ffn_conv_b, w_out_ffn):
    depth = norm_mix.shape[0]
    n_odd = w_uv_c.shape[0]
    bp, seq, _ = x_prompt.shape
    bs, dseq, _ = x_sample.shape

    def per_layer(a):
        return [a[i].astype(BF16) for i in range(a.shape[0])]

    common = dict(
        norm_mix=norm_mix, norm_ffn=norm_ffn, norm_final=norm_final, lb_logits=hgrn_lb_logits,
        w_in_ab=per_layer(w_in_ab), hgrn_out_gain=hgrn_out_gain, pool_w=per_layer(pool_w),
        pool_scale=pool_scale, w_out_ab=per_layer(w_out_ab),
        w_uv=per_layer(w_uv_c), c_ln_gain=c_ln_gain, c_ln_bias=c_ln_bias, w_out_c=per_layer(w_out_c),
        w_in_ffn=per_layer(w_in_ffn), conv_w=ffn_conv_w, conv_b=ffn_conv_b[:, None, :],
        w_d=per_layer(w_out_ffn),
    )

    def gating(length, copies):
        idx = jnp.arange(length)
        mask = (idx[None, :] // C_CAUSAL) <= (idx[:, None] // C_CAUSAL)
        wm = jnp.where(mask[None, None], c_ws[:, :, :length, :length], 0.0)
        if copies > 1:
            eye = jnp.eye(copies, dtype=wm.dtype)
            wm = jnp.einsum("ab,ogts->ogatbs", eye, wm).reshape(n_odd, C_GROUPS, copies * length, copies * length)
        bias = jnp.tile(jnp.swapaxes(c_bs[:, :, :length], 1, 2), (1, copies, 1))
        return wm.astype(BF16), bias

    wm_p, bias_p = gating(min(C_BLOCK, seq), 1)
    Lp = min(PROMPT_TILE, seq)
    y_p, hg_p, pool_p, conv_p, _ = _trunk(x_prompt, None, None, None, 0, dict(common, w_mix=wm_p, c_bias=bias_p),
                                          nseq=1, L=Lp, want_v=False)

    wm_s, bias_s = gating(min(C_BLOCK, dseq), bs)
    pool_hist = jnp.pad(state_pool, ((0, 0), (0, 0), (POOL_ROWS - state_pool.shape[2], 0), (0, 0)))
    y_s, hg_s, pool_s, conv_s, cv_s = _trunk(
        x_sample.reshape(1, bs * dseq, D_MODEL), state_hgrn, pool_hist, state_ffn_conv, PAST_LEN,
        dict(common, w_mix=wm_s, c_bias=bias_s), nseq=bs, L=dseq, want_v=True)
    n_hist = state_pool.shape[2]
    cv_s = jnp.stack(cv_s).reshape(n_odd, bs, dseq, C_WIDTH)
    return (y_p, y_s.reshape(bs, dseq, D_MODEL), hg_p, hg_s, pool_p[:, :, POOL_ROWS - n_hist:],
            pool_s[:, :, POOL_ROWS - n_hist:], conv_p, conv_s, cv_s)
```

```python
import functools
from typing import NamedTuple

import numpy as np
import jax
import jax.numpy as jnp
from jax import lax
from jax.experimental import pallas as pl
from jax.experimental.pallas import tpu as pltpu

F32 = jnp.float32
BF16 = jnp.bfloat16

D_MODEL = 1024
EPS = 1e-6
PAST_LEN = 4096
SUBLANES = 8
A_HEADS = 4
A_DIM = 128
A_WIDTH = A_HEADS * A_DIM
HGRN_CHUNK = 64
LOG2_E = 1.4426950408889634
POOL_WINDOWS = (2, 4, 8, 16)
B_GROUP_DIM = 128
B_WIDTH = len(POOL_WINDOWS) * B_GROUP_DIM
POOL_ROWS = 16
AB_IN = 4 * A_WIDTH + B_WIDTH
IN_BLOCK = 256
OUT_BLOCK = 256
C_BLOCK = 128
C_CAUSAL = 64
C_GROUPS = 8
C_WIDTH = 2 * D_MODEL
C_GROUP_DIM = C_WIDTH // C_GROUPS
D_FF = 2816
FF_CHUNK = 256
N_FF_CHUNKS = D_FF // FF_CHUNK
CONV_W = 3

PROMPT_TILE = 512
V7X_VMEM_LIMIT_BYTES = 56 * 1024 * 1024


def _dot(a, b):
    return jnp.dot(a, b, preferred_element_type=F32)


def _dot_nt(a, b):
    return lax.dot_general(a, b, (((1,), (1,)), ((), ())), preferred_element_type=F32)


def _dot_tn(a, b):
    return lax.dot_general(a, b, (((0,), (0,)), ((), ())), preferred_element_type=F32)


def _rms(x, gain):
    return x * lax.rsqrt(jnp.mean(x * x, axis=-1, keepdims=True) + EPS) * gain


def _sigmoid(x):
    return 1.0 / (1.0 + jnp.exp(-x))


def _gelu(x):
    return 0.5 * x * (1.0 + lax.erf(x * 0.7071067811865476))


def _hgrn_levels(C):
    levels, size = [], 2
    while size <= C:
        levels.append(size)
        size *= 2
    return levels


def _hgrn_sum_matrix(C):
    t = np.arange(C)[:, None]
    s = np.arange(C)[None, :]
    blocks = [s <= t, s > t]
    for size in _hgrn_levels(C):
        half = size // 2
        same = (t // size) == (s // size)
        blocks.append((same & (t % size >= half) & (s % size >= half) & (s <= t))
                      | (same & (t % size < half) & (s % size < half) & (s > t)))
    blocks.append(np.ones((SUBLANES, C), bool))
    m = np.concatenate(blocks, axis=0).astype(np.float32)
    return np.concatenate([m, m, m], axis=1)


def _even_body(layer_e, n_even, nseq, L, C, pos0, has_state, *refs):
    if has_state:
        (x_ref, sin_ref, pin_ref, gain_ref, lbl_ref, win_ref, og_ref, pw_ref, ps_ref, wout_ref, m3_ref,
         xo_ref, sout_ref, pout_ref, proj_scr, h_scr, o_scr, cat_scr, ext_scr, q_scr, k_scr, e_scr, p_scr, s_scr, ph_scr) = refs
    else:
        (x_ref, gain_ref, lbl_ref, win_ref, og_ref, pw_ref, ps_ref, wout_ref, m3_ref,
         xo_ref, sout_ref, pout_ref, proj_scr, h_scr, o_scr, cat_scr, ext_scr, q_scr, k_scr, e_scr, p_scr, s_scr, ph_scr) = refs
    t = pl.program_id(1)
    Tt = nseq * L
    levels = _hgrn_levels(C)
    e_rows = e_scr.shape[1]

    @pl.when(t == 0)
    def _():
        if has_state:
            for s in range(nseq):
                for hh in range(A_HEADS):
                    s_scr[s, hh] = sin_ref[s, hh].T
            ph_scr[...] = pin_ref[...]
        else:
            s_scr[...] = jnp.zeros_like(s_scr)
            ph_scr[...] = jnp.zeros_like(ph_scr)

    lrows = [lbl_ref[pl.ds(i, 1), :] for i in range(n_even)]
    lmax = functools.reduce(jnp.maximum, lrows)
    lexp = [jnp.exp(r - lmax) for r in lrows]
    lden = functools.reduce(lambda a, b: a + b, lexp)
    lb = jnp.zeros_like(lden)
    for i in range(1, layer_e + 1):
        lb = lb + lexp[i] / lden
    log_lb = jnp.log(lb)
    log1m_lb = jnp.log1p(-lb)
    one_m_lb = 1.0 - lb

    qi = lax.broadcasted_iota(jnp.int32, (C, C), 0)
    ki = lax.broadcasted_iota(jnp.int32, (C, C), 1)
    diagonal = qi == ki
    owned = {size: ((qi // size) == (ki // size)) & ((qi & (size - 1)) >= size // 2) & ((ki & (size - 1)) < size // 2)
             for size in levels}
    chunks_per_seg = L // C

    n_chunks = Tt // C
    heads = [slice(hh * A_DIM, (hh + 1) * A_DIM) for hh in range(A_HEADS)]

    Hn = Tt // 2
    seq_h = max(nseq // 2, 1)
    L_h = Hn // seq_h
    seg_rows = POOL_ROWS + L_h
    chunks_h = n_chunks // 2

    def project_pieces(hf):
        rows = slice(hf * Hn, (hf + 1) * Hn)

        def norm():
            h_scr[hf] = _rms(x_ref[rows, :], gain_ref[...]).astype(BF16)

        def block(cb):
            cols = slice(cb * IN_BLOCK, (cb + 1) * IN_BLOCK)
            proj_scr[rows, cols] = _dot(h_scr[hf], win_ref[:, cols])

        def gates():
            qr = proj_scr[rows, 0:A_WIDTH]
            fz = proj_scr[rows, A_WIDTH:2 * A_WIDTH]
            q_scr[rows, :] = qr * _sigmoid(qr) * (A_DIM ** -0.5)
            log_sig = jnp.minimum(fz, 0.0) - jnp.log(1.0 + jnp.exp(-jnp.abs(fz)))
            y = log1m_lb + log_sig
            lf = jnp.maximum(log_lb, y) + jnp.log(1.0 + jnp.exp(-jnp.abs(log_lb - y)))
            k_scr[rows, :] = one_m_lb * _sigmoid(-fz)
            lf2 = lf * LOG2_E
            hi = lf2.astype(BF16)
            r1 = lf2 - hi.astype(F32)
            mid = r1.astype(BF16)
            lo = (r1 - mid.astype(F32)).astype(BF16)
            for ci in range(chunks_h):
                rs = slice(ci * C, (ci + 1) * C)
                e_scr[hf * chunks_h + ci] = _dot(m3_ref[...], jnp.concatenate([hi[rs], mid[rs], lo[rs]], axis=0))

        return [norm] + [functools.partial(block, cb) for cb in range(AB_IN // IN_BLOCK)] + [gates]

    def scores(c, hh):
        rs, cs = slice(c * C, (c + 1) * C), heads[hh]
        q_b, k_b = q_scr[rs, cs].astype(BF16), k_scr[rs, cs].astype(BF16)
        p = jnp.where(diagonal, _dot_nt(q_b, k_b), 0.0)
        for li, size in enumerate(levels):
            r0 = (2 + li) * C
            decay = jnp.exp2(e_scr[c, r0:r0 + C, cs]).astype(BF16)
            p = jnp.where(owned[size], _dot_nt(q_b * decay, k_b * decay), p)
        p_scr[c % 2, hh] = p.astype(BF16)

    def carried(c, hh):
        rs, cs = slice(c * C, (c + 1) * C), heads[hh]
        seg = c // chunks_per_seg
        state_t = s_scr[seg, hh]
        qs = (q_scr[rs, cs] * jnp.exp2(e_scr[c, 0:C, cs])).astype(BF16)
        ks = (k_scr[rs, cs] * jnp.exp2(e_scr[c, C:2 * C, cs])).astype(BF16)
        o_scr[rs, cs] = _dot_nt(qs, state_t.astype(BF16))
        dec = jnp.exp2(e_scr[c, e_rows - SUBLANES:e_rows - SUBLANES + 1, cs])
        s_scr[seg, hh] = state_t * dec + _dot_tn(proj_scr[rs, 2 * A_WIDTH + hh * A_DIM:
                                                          2 * A_WIDTH + (hh + 1) * A_DIM].astype(BF16), ks)

    def weighted(c, hh):
        rs, cs = slice(c * C, (c + 1) * C), heads[hh]
        v = proj_scr[rs, 2 * A_WIDTH + hh * A_DIM:2 * A_WIDTH + (hh + 1) * A_DIM].astype(BF16)
        o_scr[rs, cs] += _dot(p_scr[c % 2, hh], v)

    def recur_pieces(hf, ahead=True):
        pieces = []
        for c in range(hf * chunks_h, (hf + 1) * chunks_h):
            last = c + 1 == (hf + 1) * chunks_h
            pieces += [functools.partial(carried, c, hh) for hh in range(A_HEADS)]
            if c + 1 < n_chunks and (ahead or not last):
                pieces += [functools.partial(scores, c + 1, hh) for hh in range(A_HEADS)]
            pieces += [functools.partial(weighted, c, hh) for hh in range(A_HEADS)]
        return pieces

    def emit_pieces(hf):
        rows = slice(hf * Hn, (hf + 1) * Hn)
        seq0 = hf * seq_h if nseq > 1 else 0

        def history():
            for s in range(seq_h):
                ext_scr[pl.ds(s * seg_rows, POOL_ROWS), :] = ph_scr[seq0 + s]
                ext_scr[pl.ds(s * seg_rows + POOL_ROWS, L_h), :] = proj_scr[pl.ds(hf * Hn + s * L_h, L_h),
                                                                            4 * A_WIDTH:AB_IN]
            for s in range(seq_h):
                ph_scr[seq0 + s] = ext_scr[pl.ds(s * seg_rows + L_h, POOL_ROWS), :]

        def new_rows(a):
            if seq_h == 1:
                return a[POOL_ROWS:]
            return a.reshape(seq_h, seg_rows, a.shape[-1])[:, POOL_ROWS:, :].reshape(Hn, a.shape[-1])

        def pool(gi):
            w = POOL_WINDOWS[gi]
            c0 = gi * B_GROUP_DIM
            first_pos = pos0 + t * L + (hf * L_h if nseq == 1 else 0)
            pos = first_pos + (lax.broadcasted_iota(jnp.int32, (Hn, 1), 0) & (L_h - 1))
            cur = ext_scr[:, c0:c0 + B_GROUP_DIM]
            acc = cur
            sh = 1
            while sh < w:
                acc = acc + pltpu.roll(acc, sh, 0)
                sh *= 2
            cnt = jnp.minimum(pos + 1, w).astype(F32)
            pooled = new_rows(acc) / cnt - new_rows(cur)
            yb = _dot(pooled.astype(BF16), pw_ref[gi]) * ps_ref[:, c0:c0 + B_GROUP_DIM]
            cat_scr[rows, A_WIDTH + c0:A_WIDTH + c0 + B_GROUP_DIM] = yb.astype(BF16)

        def gate(hh):
            c0 = hh * A_DIM
            oh = o_scr[rows, c0:c0 + A_DIM]
            g = proj_scr[rows, 3 * A_WIDTH + c0:3 * A_WIDTH + c0 + A_DIM]
            cat_scr[rows, c0:c0 + A_DIM] = (_rms(oh, og_ref[...]) * (g * _sigmoid(g))).astype(BF16)

        def block(cb):
            cols = slice(cb * OUT_BLOCK, (cb + 1) * OUT_BLOCK)
            xo_ref[rows, cols] = x_ref[rows, cols] + _dot(cat_scr[rows, :], wout_ref[:, cols])

        return ([history] + [functools.partial(pool, gi) for gi in range(len(POOL_WINDOWS))]
                + [functools.partial(gate, hh) for hh in range(A_HEADS)]
                + [functools.partial(block, cb) for cb in range(D_MODEL // OUT_BLOCK)])

    def interleave(main, side):
        done = 0
        for i, piece in enumerate(main):
            due = (i + 1) * len(side) // len(main)
            for extra in side[done:due]:
                extra()
            done = due
            piece()

    def run(pieces):
        for piece in pieces:
            piece()

    run(project_pieces(0))
    run([functools.partial(scores, 0, hh) for hh in range(A_HEADS)])
    interleave(recur_pieces(0, ahead=False), project_pieces(1))
    run([functools.partial(scores, chunks_h, hh) for hh in range(A_HEADS)])
    interleave(recur_pieces(1), emit_pieces(0))
    run(emit_pieces(1))

    for s in range(nseq):
        for hh in range(A_HEADS):
            sout_ref[s, hh] = s_scr[s, hh].T
    pout_ref[...] = ph_scr[...]


class _Layer(NamedTuple):
    stacked: jax.Array
    index: int

    @property
    def shape(self):
        return self.stacked.shape[1:]


def _weight_spec(w):
    if isinstance(w, _Layer):
        index = (w.index,) + (0,) * len(w.shape)
        return pl.BlockSpec((None,) + tuple(w.shape), lambda g, t: index, pipeline_mode=pl.Buffered(1))
    zeros = (0,) * w.ndim
    return pl.BlockSpec(w.shape, lambda g, t: zeros, pipeline_mode=pl.Buffered(1))


def _weight_args(weights):
    return [w.stacked if isinstance(w, _Layer) else w for w in weights]


def _even_layer(layer_e, n_even, x, s_hgrn, s_pool, gain, lb_logits, w_in, out_gain, pool_w, pool_scale, w_out,
                *, nseq, L, C, pos0):
    G, T, _ = x.shape
    Tt = nseq * L
    has_state = s_hgrn is not None
    body = functools.partial(_even_body, layer_e, n_even, nseq, L, C, pos0, has_state)
    x_spec = pl.BlockSpec((None, Tt, D_MODEL), lambda g, t: (g, t, 0))
    s_spec = pl.BlockSpec((nseq, A_HEADS, A_DIM, A_DIM), lambda g, t: (g, 0, 0, 0))
    p_spec = pl.BlockSpec((nseq, POOL_ROWS, B_WIDTH), lambda g, t: (g, 0, 0))
    m3 = jnp.asarray(_hgrn_sum_matrix(C), BF16)
    weights = (gain, lb_logits, w_in, out_gain, pool_w, pool_scale, w_out, m3)
    w_specs = [_weight_spec(w) for w in weights]
    state_in = (s_hgrn, s_pool) if has_state else ()
    state_specs = [s_spec, p_spec] if has_state else []
    return pl.pallas_call(
        body,
        grid=(G, T // Tt),
        in_specs=[x_spec] + state_specs + w_specs,
        out_specs=(x_spec, s_spec, p_spec),
        out_shape=(jax.ShapeDtypeStruct(x.shape, F32),
                   jax.ShapeDtypeStruct((G * nseq, A_HEADS, A_DIM, A_DIM), F32),
                   jax.ShapeDtypeStruct((G * nseq, POOL_ROWS, B_WIDTH), F32)),
        scratch_shapes=[pltpu.VMEM((Tt, AB_IN), F32), pltpu.VMEM((2, Tt // 2, D_MODEL), BF16),
                        pltpu.VMEM((Tt, A_WIDTH), F32), pltpu.VMEM((Tt, A_WIDTH + B_WIDTH), BF16),
                        pltpu.VMEM((max(nseq // 2, 1) * POOL_ROWS + Tt // 2, B_WIDTH), F32),
                        pltpu.VMEM((Tt, A_WIDTH), F32), pltpu.VMEM((Tt, A_WIDTH), F32),
                        pltpu.VMEM((Tt // C, m3.shape[0], A_WIDTH), F32),
                        pltpu.VMEM((2, A_HEADS, C, C), BF16),
                        pltpu.VMEM((nseq, A_HEADS, A_DIM, A_DIM), F32),
                        pltpu.VMEM((nseq, POOL_ROWS, B_WIDTH), F32)],
        compiler_params=pltpu.CompilerParams(dimension_semantics=("arbitrary", "arbitrary"),
                                             vmem_limit_bytes=V7X_VMEM_LIMIT_BYTES),
        name=f"even_mixer_{layer_e}_{'s' if has_state else 'p'}",
    )(x, *state_in, *_weight_args(weights))


def _odd_body(Tt, Lc, want_v, *refs):
    if want_v:
        (x_ref, gain_ref, wuv_ref, lng_ref, lnb_ref, wmix_ref, bias_ref, wout_ref,
         xo_ref, vo_ref, zv_scr, u_scr, vn_scr, m_scr, acc_scr) = refs
    else:
        (x_ref, gain_ref, wuv_ref, lng_ref, lnb_ref, wmix_ref, bias_ref, wout_ref,
         xo_ref, zv_scr, u_scr, vn_scr, m_scr, acc_scr) = refs
    x = x_ref[...]
    h = _rms(x, gain_ref[...]).astype(BF16)
    zv_scr[...] = _dot(h, wuv_ref[:, C_WIDTH:])
    for g in range(C_GROUPS):
        u_scr[g] = _dot(h, wuv_ref[:, g * C_GROUP_DIM:(g + 1) * C_GROUP_DIM])
    zv = _gelu(zv_scr[...])
    mu = jnp.mean(zv, axis=-1, keepdims=True)
    zc = zv - mu
    vn = zc * lax.rsqrt(jnp.mean(zc * zc, axis=-1, keepdims=True) + EPS) * lng_ref[...] + lnb_ref[...]
    if want_v:
        vo_ref[...] = vn
    vn_scr[...] = vn.astype(BF16)

    def gate_rows(g):
        c0 = g * C_GROUP_DIM
        bias = bias_ref[:, g:g + 1]
        for b in range(Tt // Lc):
            m_scr[g % 2, pl.ds(b * Lc, Lc), :] = (
                _dot(wmix_ref[g], vn_scr[pl.ds(b * Lc, Lc), c0:c0 + C_GROUP_DIM]) + bias)

    gate_rows(0)
    for g in range(C_GROUPS):
        if g + 1 < C_GROUPS:
            gate_rows(g + 1)
        y = (_gelu(u_scr[g]) * m_scr[g % 2]).astype(BF16)
        part = _dot(y, wout_ref[g * C_GROUP_DIM:(g + 1) * C_GROUP_DIM, :])
        if g == 0:
            acc_scr[...] = part
        else:
            acc_scr[...] += part
    xo_ref[...] = x + acc_scr[...]


def _odd_layer(x, gain, w_uv, ln_gain, ln_bias, w_mix, bias, w_out, *, Tt, want_v):
    G, T, _ = x.shape
    Lc = w_mix.shape[-1]
    body = functools.partial(_odd_body, Tt, Lc, want_v)
    x_spec = pl.BlockSpec((None, Tt, D_MODEL), lambda g, t: (g, t, 0))
    weights = (gain, w_uv, ln_gain, ln_bias, w_mix, bias, w_out)
    out_specs = [x_spec]
    out_shape = [jax.ShapeDtypeStruct(x.shape, F32)]
    if want_v:
        out_specs.append(pl.BlockSpec((None, Tt, C_WIDTH), lambda g, t: (g, t, 0)))
        out_shape.append(jax.ShapeDtypeStruct((G, T, C_WIDTH), F32))
    return pl.pallas_call(
        body,
        grid=(G, T // Tt),
        in_specs=[x_spec] + [_weight_spec(w) for w in weights],
        out_specs=tuple(out_specs),
        out_shape=tuple(out_shape),
        scratch_shapes=[pltpu.VMEM((Tt, C_WIDTH), F32), pltpu.VMEM((C_GROUPS, Tt, C_GROUP_DIM), F32),
                        pltpu.VMEM((Tt, C_WIDTH), BF16), pltpu.VMEM((2, Tt, C_GROUP_DIM), F32),
                        pltpu.VMEM((Tt, D_MODEL), F32)],
        compiler_params=pltpu.CompilerParams(dimension_semantics=("arbitrary", "arbitrary"),
                                             vmem_limit_bytes=V7X_VMEM_LIMIT_BYTES),
        name=f"odd_mixer_{'s' if want_v else 'p'}",
    )(x, *_weight_args(weights))


def _ffn_body(nseq, L, has_state, final_norm, *refs):
    refs = list(refs)
    x_ref = refs.pop(0)
    hin_ref = refs.pop(0) if has_state else None
    gain_ref, win_ref, cw_ref, cb_ref, wd_ref = refs[:5]
    refs = refs[5:]
    gf_ref = refs.pop(0) if final_norm else None
    xo_ref, hout_ref, acc_scr, hist_scr, ag_scr, y_scr = refs
    t = pl.program_id(1)
    Tt = nseq * L

    @pl.when(t == 0)
    def _():
        if has_state:
            hist_scr[0] = hin_ref[:, 0, :]
            hist_scr[1] = hin_ref[:, 1, :]
        else:
            hist_scr[...] = jnp.zeros_like(hist_scr)

    x = x_ref[...]
    h = _rms(x, gain_ref[...]).astype(BF16)
    rowm = lax.broadcasted_iota(jnp.int32, (Tt, 1), 0) & (L - 1)

    def expand(rows):
        if nseq == 1:
            return rows
        return jnp.broadcast_to(rows[:, None, :], (nseq, L, rows.shape[-1])).reshape(Tt, rows.shape[-1])

    def up(j):
        for half, c0 in enumerate((j * FF_CHUNK, D_FF + j * FF_CHUNK)):
            ag_scr[j % 2, :, :, half * FF_CHUNK:(half + 1) * FF_CHUNK] = (
                _dot(h, win_ref[:, c0:c0 + FF_CHUNK]).reshape(nseq, L, FF_CHUNK))

    def down(j):
        part = _dot(y_scr[j % 2], wd_ref[j * FF_CHUNK:(j + 1) * FF_CHUNK, :])
        if j == 0:
            acc_scr[...] = part
        else:
            acc_scr[...] += part

    up(0)
    for j in range(N_FF_CHUNKS):
        cols = slice(j * FF_CHUNK, (j + 1) * FF_CHUNK)
        slot = j % 2
        if j + 1 < N_FF_CHUNKS:
            up(j + 1)
        a = ag_scr[slot, :, :, :FF_CHUNK].reshape(Tt, FF_CHUNK)
        gate = ag_scr[slot, :, :, FF_CHUNK:].reshape(Tt, FF_CHUNK)
        h2 = expand(hist_scr[0, :, cols])
        h1 = expand(hist_scr[1, :, cols])
        a1 = jnp.where(rowm >= 1, pltpu.roll(a, 1, 0), h1)
        a2 = jnp.where(rowm >= 2, pltpu.roll(a, 2, 0), jnp.where(rowm == 1, h1, h2))
        conv = (cb_ref[:, cols] + a2 * cw_ref[pl.ds(0, 1), cols] + a1 * cw_ref[pl.ds(1, 1), cols]
                + a * cw_ref[pl.ds(2, 1), cols])
        y_scr[slot] = (_gelu(conv) * gate).astype(BF16)
        hist_scr[0, :, cols] = ag_scr[slot, :, L - 2, :FF_CHUNK]
        hist_scr[1, :, cols] = ag_scr[slot, :, L - 1, :FF_CHUNK]
        if j > 0:
            down(j - 1)
    down(N_FF_CHUNKS - 1)
    xn = x + acc_scr[...]
    if final_norm:
        xn = _rms(xn, gf_ref[...])
    xo_ref[...] = xn
    hout_ref[:, 0, :] = hist_scr[0]
    hout_ref[:, 1, :] = hist_scr[1]


def _ffn_layer(x, s_conv, gain, w_in, conv_w, conv_b, w_d, final_gain, *, nseq, L):
    G, T, _ = x.shape
    Tt = nseq * L
    has_state = s_conv is not None
    final_norm = final_gain is not None
    body = functools.partial(_ffn_body, nseq, L, has_state, final_norm)
    x_spec = pl.BlockSpec((None, Tt, D_MODEL), lambda g, t: (g, t, 0))
    h_spec = pl.BlockSpec((nseq, CONV_W - 1, D_FF), lambda g, t: (g, 0, 0))
    weights = (gain, w_in, conv_w, conv_b, w_d) + ((final_gain,) if final_norm else ())
    state_in = (s_conv,) if has_state else ()
    return pl.pallas_call(
        body,
        grid=(G, T // Tt),
        in_specs=[x_spec] + ([h_spec] if has_state else []) + [_weight_spec(w) for w in weights],
        out_specs=(x_spec, h_spec),
        out_shape=(jax.ShapeDtypeStruct(x.shape, F32),
                   jax.ShapeDtypeStruct((G * nseq, CONV_W - 1, D_FF), F32)),
        scratch_shapes=[pltpu.VMEM((Tt, D_MODEL), F32), pltpu.VMEM((CONV_W - 1, nseq, D_FF), F32),
                        pltpu.VMEM((2, nseq, L, 2 * FF_CHUNK), F32),
                        pltpu.VMEM((2, Tt, FF_CHUNK), BF16)],
        compiler_params=pltpu.CompilerParams(dimension_semantics=("arbitrary", "arbitrary"),
                                             vmem_limit_bytes=V7X_VMEM_LIMIT_BYTES),
        name=f"conv_ffn_{'s' if has_state else 'p'}{'_final' if final_norm else ''}",
    )(x, *state_in, *_weight_args(weights))


def _trunk(x, s_hgrn, s_pool, s_conv, pos0, w, *, nseq, L, want_v):
    depth = w["norm_mix"].shape[0]
    n_even = w["lb_logits"].shape[0]
    Tt = nseq * L
    C = min(HGRN_CHUNK, L)
    new_hgrn, new_pool, new_conv, new_cv = [], [], [], []
    for layer in range(depth):
        if layer % 2 == 0:
            e = layer // 2
            x, sh, sp = _even_layer(
                e, n_even, x, None if s_hgrn is None else s_hgrn[e], None if s_pool is None else s_pool[e],
                w["norm_mix"][layer][None], w["lb_logits"], _Layer(w["w_in_ab"], e), w["hgrn_out_gain"][e][None],
                _Layer(w["pool_w"], e), w["pool_scale"][e][None], _Layer(w["w_out_ab"], e),
                nseq=nseq, L=L, C=C, pos0=pos0)
            new_hgrn.append(sh)
            new_pool.append(sp)
        else:
            o = layer // 2
            out = _odd_layer(x, w["norm_mix"][layer][None], _Layer(w["w_uv"], o), w["c_ln_gain"][o][None],
                             w["c_ln_bias"][o][None], _Layer(w["w_mix"], o), _Layer(w["c_bias"], o),
                             _Layer(w["w_out_c"], o),
                             Tt=Tt, want_v=want_v)
            x = out[0]
            if want_v:
                new_cv.append(out[1])
        final_gain = w["norm_final"][None] if layer == depth - 1 else None
        x, sc = _ffn_layer(x, None if s_conv is None else s_conv[layer], w["norm_ffn"][layer][None],
                           _Layer(w["w_in_ffn"], layer), _Layer(w["conv_w"], layer), _Layer(w["conv_b"], layer),
                           _Layer(w["w_d"], layer), final_gain, nseq=nseq, L=L)
        new_conv.append(sc)
    return x, jnp.stack(new_hgrn), jnp.stack(new_pool), jnp.stack(new_conv), new_cv


def kernel(x_prompt, x_sample, state_hgrn, state_pool, state_ffn_conv, norm_mix, norm_ffn, norm_final, hgrn_lb_logits, w_in_ab, hgrn_out_gain, pool_w, pool_scale, w_out_ab, w_uv_c, c_ln_gain, c_ln_bias, c_ws, c_bs, w_out_c, w_in_ffn, ffn_conv_w, ffn_conv_b, w_out_ffn):
    depth = norm_mix.shape[0]
    n_odd = w_uv_c.shape[0]
    bp, seq, _ = x_prompt.shape
    bs, dseq, _ = x_sample.shape

    common = dict(
        norm_mix=norm_mix, norm_ffn=norm_ffn, norm_final=norm_final, lb_logits=hgrn_lb_logits,
        w_in_ab=w_in_ab.astype(BF16), hgrn_out_gain=hgrn_out_gain, pool_w=pool_w.astype(BF16),
        pool_scale=pool_scale, w_out_ab=w_out_ab.astype(BF16),
        w_uv=w_uv_c.astype(BF16), c_ln_gain=c_ln_gain, c_ln_bias=c_ln_bias, w_out_c=w_out_c.astype(BF16),
        w_in_ffn=w_in_ffn.astype(BF16), conv_w=ffn_conv_w, conv_b=ffn_conv_b[:, None, :],
        w_d=w_out_ffn.astype(BF16),
    )

    def gating(length, copies):
        idx = jnp.arange(length)
        mask = (idx[None, :] // C_CAUSAL) <= (idx[:, None] // C_CAUSAL)
        wm = jnp.where(mask[None, None], c_ws[:, :, :length, :length], 0.0)
        if copies > 1:
            eye = jnp.eye(copies, dtype=wm.dtype)
            wm = jnp.einsum("ab,ogts->ogatbs", eye, wm).reshape(n_odd, C_GROUPS, copies * length, copies * length)
        bias = jnp.tile(jnp.swapaxes(c_bs[:, :, :length], 1, 2), (1, copies, 1))
        return wm.astype(BF16), bias

    wm_p, bias_p = gating(min(C_BLOCK, seq), 1)
    Lp = min(PROMPT_TILE, seq)
    y_p, hg_p, pool_p, conv_p, _ = _trunk(x_prompt, None, None, None, 0, dict(common, w_mix=wm_p, c_bias=bias_p),
                                          nseq=1, L=Lp, want_v=False)

    wm_s, bias_s = gating(min(C_BLOCK, dseq), bs)
    pool_hist = jnp.pad(state_pool, ((0, 0), (0, 0), (POOL_ROWS - state_pool.shape[2], 0), (0, 0)))
    y_s, hg_s, pool_s, conv_s, cv_s = _trunk(
        x_sample.reshape(1, bs * dseq, D_MODEL), state_hgrn, pool_hist, state_ffn_conv, PAST_LEN,
        dict(common, w_mix=wm_s, c_bias=bias_s), nseq=bs, L=dseq, want_v=True)
    n_hist = state_pool.shape[2]
    cv_s = jnp.stack(cv_s).reshape(n_odd, bs, dseq, C_WIDTH)
    return (y_p, y_s.reshape(bs, dseq, D_MODEL), hg_p, hg_s, pool_p[:, :, POOL_ROWS - n_hist:],
            pool_s[:, :, POOL_ROWS - n_hist:], conv_p, conv_s, cv_s)
```

```python
import functools
from typing import NamedTuple

import numpy as np
import jax
import jax.numpy as jnp
from jax import lax
from jax.experimental import pallas as pl
from jax.experimental.pallas import tpu as pltpu

F32 = jnp.float32
BF16 = jnp.bfloat16

D_MODEL = 1024
EPS = 1e-6
PAST_LEN = 4096
SUBLANES = 8
A_HEADS = 4
A_DIM = 128
A_WIDTH = A_HEADS * A_DIM
HGRN_CHUNK = 64
LOG2_E = 1.4426950408889634
POOL_WINDOWS = (2, 4, 8, 16)
B_GROUP_DIM = 128
B_WIDTH = len(POOL_WINDOWS) * B_GROUP_DIM
POOL_ROWS = 16
AB_IN = 4 * A_WIDTH + B_WIDTH
IN_BLOCK = 256
OUT_BLOCK = 256
C_BLOCK = 128
C_CAUSAL = 64
C_GROUPS = 8
C_WIDTH = 2 * D_MODEL
C_GROUP_DIM = C_WIDTH // C_GROUPS
D_FF = 2816
FF_CHUNK = 256
N_FF_CHUNKS = D_FF // FF_CHUNK
CONV_W = 3

PROMPT_TILE = 512
V7X_VMEM_LIMIT_BYTES = 56 * 1024 * 1024


def _dot(a, b):
    return jnp.dot(a, b, preferred_element_type=F32)


def _dot_nt(a, b):
    return lax.dot_general(a, b, (((1,), (1,)), ((), ())), preferred_element_type=F32)


def _dot_tn(a, b):
    return lax.dot_general(a, b, (((0,), (0,)), ((), ())), preferred_element_type=F32)


def _rms(x, gain):
    return x * lax.rsqrt(jnp.mean(x * x, axis=-1, keepdims=True) + EPS) * gain


def _sigmoid(x):
    return 1.0 / (1.0 + jnp.exp(-x))


def _gelu(x):
    return 0.5 * x * (1.0 + lax.erf(x * 0.7071067811865476))


def _hgrn_levels(C):
    levels, size = [], 2
    while size <= C:
        levels.append(size)
        size *= 2
    return levels


def _hgrn_sum_matrix(C):
    t = np.arange(C)[:, None]
    s = np.arange(C)[None, :]
    blocks = [s <= t, s > t]
    for size in _hgrn_levels(C):
        half = size // 2
        same = (t // size) == (s // size)
        blocks.append((same & (t % size >= half) & (s % size >= half) & (s <= t))
                      | (same & (t % size < half) & (s % size < half) & (s > t)))
    blocks.append(np.ones((SUBLANES, C), bool))
    m = np.concatenate(blocks, axis=0).astype(np.float32)
    return np.concatenate([m, m, m], axis=1)


def _even_body(layer_e, n_even, nseq, L, C, pos0, has_state, *refs):
    if has_state:
        (x_ref, sin_ref, pin_ref, gain_ref, lbl_ref, win_ref, og_ref, pw_ref, ps_ref, wout_ref, m3_ref,
         xo_ref, sout_ref, pout_ref, proj_scr, h_scr, o_scr, cat_scr, ext_scr, q_scr, k_scr, e_scr, p_scr, kt_scr, skv_scr, s_scr, ph_scr) = refs
    else:
        (x_ref, gain_ref, lbl_ref, win_ref, og_ref, pw_ref, ps_ref, wout_ref, m3_ref,
         xo_ref, sout_ref, pout_ref, proj_scr, h_scr, o_scr, cat_scr, ext_scr, q_scr, k_scr, e_scr, p_scr, kt_scr, skv_scr, s_scr, ph_scr) = refs
    t = pl.program_id(1)
    Tt = nseq * L
    levels = _hgrn_levels(C)
    e_rows = e_scr.shape[1]

    @pl.when(t == 0)
    def _():
        if has_state:
            for s in range(nseq):
                for hh in range(A_HEADS):
                    s_scr[s, hh] = sin_ref[s, hh].T
            ph_scr[...] = pin_ref[...]
        else:
            s_scr[...] = jnp.zeros_like(s_scr)
            ph_scr[...] = jnp.zeros_like(ph_scr)

    lrows = [lbl_ref[pl.ds(i, 1), :] for i in range(n_even)]
    lmax = functools.reduce(jnp.maximum, lrows)
    lexp = [jnp.exp(r - lmax) for r in lrows]
    lden = functools.reduce(lambda a, b: a + b, lexp)
    lb = jnp.zeros_like(lden)
    for i in range(1, layer_e + 1):
        lb = lb + lexp[i] / lden
    log_lb = jnp.log(lb)
    log1m_lb = jnp.log1p(-lb)
    one_m_lb = 1.0 - lb

    qi = lax.broadcasted_iota(jnp.int32, (C, C), 0)
    ki = lax.broadcasted_iota(jnp.int32, (C, C), 1)
    diagonal = qi == ki
    owned = {size: ((qi // size) == (ki // size)) & ((qi & (size - 1)) >= size // 2) & ((ki & (size - 1)) < size // 2)
             for size in levels}
    chunks_per_seg = L // C

    n_chunks = Tt // C
    heads = [slice(hh * A_DIM, (hh + 1) * A_DIM) for hh in range(A_HEADS)]

    Hn = Tt // 2
    seq_h = max(nseq // 2, 1)
    L_h = Hn // seq_h
    seg_rows = POOL_ROWS + L_h
    chunks_h = n_chunks // 2

    def project_pieces(hf):
        rows = slice(hf * Hn, (hf + 1) * Hn)

        def norm():
            h_scr[hf] = _rms(x_ref[rows, :], gain_ref[...]).astype(BF16)

        def block(cb):
            cols = slice(cb * IN_BLOCK, (cb + 1) * IN_BLOCK)
            proj_scr[rows, cols] = _dot(h_scr[hf], win_ref[:, cols])

        def gates():
            qr = proj_scr[rows, 0:A_WIDTH]
            fz = proj_scr[rows, A_WIDTH:2 * A_WIDTH]
            q_scr[rows, :] = qr * _sigmoid(qr) * (A_DIM ** -0.5)
            log_sig = jnp.minimum(fz, 0.0) - jnp.log(1.0 + jnp.exp(-jnp.abs(fz)))
            y = log1m_lb + log_sig
            lf = jnp.maximum(log_lb, y) + jnp.log(1.0 + jnp.exp(-jnp.abs(log_lb - y)))
            k_scr[rows, :] = one_m_lb * _sigmoid(-fz)
            lf2 = lf * LOG2_E
            hi = lf2.astype(BF16)
            r1 = lf2 - hi.astype(F32)
            mid = r1.astype(BF16)
            lo = (r1 - mid.astype(F32)).astype(BF16)
            for ci in range(chunks_h):
                rs = slice(ci * C, (ci + 1) * C)
                e_scr[hf * chunks_h + ci] = _dot(m3_ref[...], jnp.concatenate([hi[rs], mid[rs], lo[rs]], axis=0))

        return [norm] + [functools.partial(block, cb) for cb in range(AB_IN // IN_BLOCK)] + [gates]

    def scores(c, hh):
        rs, cs = slice(c * C, (c + 1) * C), heads[hh]
        q_b, k_b = q_scr[rs, cs].astype(BF16), k_scr[rs, cs].astype(BF16)
        pad = jnp.zeros((A_DIM - C, A_DIM), BF16) if C < A_DIM else None

        def keys_t(k, slot):
            if pad is not None:
                k = jnp.concatenate([k, pad], axis=0)
            kt_scr[slot] = k.T
            return kt_scr[slot, :, :C]

        p = jnp.where(diagonal, _dot(q_b, keys_t(k_b, 0)), 0.0)
        for li, size in enumerate(levels):
            r0 = (2 + li) * C
            decay = jnp.exp2(e_scr[c, r0:r0 + C, cs]).astype(BF16)
            p = jnp.where(owned[size], _dot(q_b * decay, keys_t(k_b * decay, li + 1)), p)
        p_scr[c % 2, hh] = p.astype(BF16)

    def carried(c, hh):
        rs, cs = slice(c * C, (c + 1) * C), heads[hh]
        seg = c // chunks_per_seg
        state_t = s_scr[seg, hh]
        qs = (q_scr[rs, cs] * jnp.exp2(e_scr[c, 0:C, cs])).astype(BF16)
        ks = (k_scr[rs, cs] * jnp.exp2(e_scr[c, C:2 * C, cs])).astype(BF16)
        skv_scr[hh % 2] = state_t.astype(BF16).T
        o_scr[rs, cs] = _dot(qs, skv_scr[hh % 2])
        dec = jnp.exp2(e_scr[c, e_rows - SUBLANES:e_rows - SUBLANES + 1, cs])
        s_scr[seg, hh] = state_t * dec + _dot_tn(proj_scr[rs, 2 * A_WIDTH + hh * A_DIM:
                                                          2 * A_WIDTH + (hh + 1) * A_DIM].astype(BF16), ks)

    def weighted(c, hh):
        rs, cs = slice(c * C, (c + 1) * C), heads[hh]
        v = proj_scr[rs, 2 * A_WIDTH + hh * A_DIM:2 * A_WIDTH + (hh + 1) * A_DIM].astype(BF16)
        o_scr[rs, cs] += _dot(p_scr[c % 2, hh], v)

    def recur_pieces(hf, ahead=True):
        pieces = []
        for c in range(hf * chunks_h, (hf + 1) * chunks_h):
            last = c + 1 == (hf + 1) * chunks_h
            pieces += [functools.partial(carried, c, hh) for hh in range(A_HEADS)]
            if c + 1 < n_chunks and (ahead or not last):
                pieces += [functools.partial(scores, c + 1, hh) for hh in range(A_HEADS)]
            pieces += [functools.partial(weighted, c, hh) for hh in range(A_HEADS)]
        return pieces

    def emit_pieces(hf):
        rows = slice(hf * Hn, (hf + 1) * Hn)
        seq0 = hf * seq_h if nseq > 1 else 0

        def history():
            for s in range(seq_h):
                ext_scr[pl.ds(s * seg_rows, POOL_ROWS), :] = ph_scr[seq0 + s]
                ext_scr[pl.ds(s * seg_rows + POOL_ROWS, L_h), :] = proj_scr[pl.ds(hf * Hn + s * L_h, L_h),
                                                                            4 * A_WIDTH:AB_IN]
            for s in range(seq_h):
                ph_scr[seq0 + s] = ext_scr[pl.ds(s * seg_rows + L_h, POOL_ROWS), :]

        def new_rows(a):
            if seq_h == 1:
                return a[POOL_ROWS:]
            return a.reshape(seq_h, seg_rows, a.shape[-1])[:, POOL_ROWS:, :].reshape(Hn, a.shape[-1])

        def pool(gi):
            w = POOL_WINDOWS[gi]
            c0 = gi * B_GROUP_DIM
            first_pos = pos0 + t * L + (hf * L_h if nseq == 1 else 0)
            pos = first_pos + (lax.broadcasted_iota(jnp.int32, (Hn, 1), 0) & (L_h - 1))
            cur = ext_scr[:, c0:c0 + B_GROUP_DIM]
            acc = cur
            sh = 1
            while sh < w:
                acc = acc + pltpu.roll(acc, sh, 0)
                sh *= 2
            cnt = jnp.minimum(pos + 1, w).astype(F32)
            pooled = new_rows(acc) / cnt - new_rows(cur)
            yb = _dot(pooled.astype(BF16), pw_ref[gi]) * ps_ref[:, c0:c0 + B_GROUP_DIM]
            cat_scr[rows, A_WIDTH + c0:A_WIDTH + c0 + B_GROUP_DIM] = yb.astype(BF16)

        def gate(hh):
            c0 = hh * A_DIM
            oh = o_scr[rows, c0:c0 + A_DIM]
            g = proj_scr[rows, 3 * A_WIDTH + c0:3 * A_WIDTH + c0 + A_DIM]
            cat_scr[rows, c0:c0 + A_DIM] = (_rms(oh, og_ref[...]) * (g * _sigmoid(g))).astype(BF16)

        def block(cb):
            cols = slice(cb * OUT_BLOCK, (cb + 1) * OUT_BLOCK)
            xo_ref[rows, cols] = x_ref[rows, cols] + _dot(cat_scr[rows, :], wout_ref[:, cols])

        return ([history] + [functools.partial(pool, gi) for gi in range(len(POOL_WINDOWS))]
                + [functools.partial(gate, hh) for hh in range(A_HEADS)]
                + [functools.partial(block, cb) for cb in range(D_MODEL // OUT_BLOCK)])

    def interleave(main, side):
        done = 0
        for i, piece in enumerate(main):
            due = (i + 1) * len(side) // len(main)
            for extra in side[done:due]:
                extra()
            done = due
            piece()

    def run(pieces):
        for piece in pieces:
            piece()

    run(project_pieces(0))
    run([functools.partial(scores, 0, hh) for hh in range(A_HEADS)])
    interleave(recur_pieces(0, ahead=False), project_pieces(1))
    run([functools.partial(scores, chunks_h, hh) for hh in range(A_HEADS)])
    interleave(recur_pieces(1), emit_pieces(0))
    run(emit_pieces(1))

    for s in range(nseq):
        for hh in range(A_HEADS):
            sout_ref[s, hh] = s_scr[s, hh].T
    pout_ref[...] = ph_scr[...]


class _Layer(NamedTuple):
    stacked: jax.Array
    index: int

    @property
    def shape(self):
        return self.stacked.shape[1:]


def _weight_spec(w):
    if isinstance(w, _Layer):
        index = (w.index,) + (0,) * len(w.shape)
        return pl.BlockSpec((None,) + tuple(w.shape), lambda g, t: index, pipeline_mode=pl.Buffered(1))
    zeros = (0,) * w.ndim
    return pl.BlockSpec(w.shape, lambda g, t: zeros, pipeline_mode=pl.Buffered(1))


def _weight_args(weights):
    return [w.stacked if isinstance(w, _Layer) else w for w in weights]


def _even_layer(layer_e, n_even, x, s_hgrn, s_pool, gain, lb_logits, w_in, out_gain, pool_w, pool_scale, w_out,
                *, nseq, L, C, pos0):
    G, T, _ = x.shape
    Tt = nseq * L
    has_state = s_hgrn is not None
    body = functools.partial(_even_body, layer_e, n_even, nseq, L, C, pos0, has_state)
    x_spec = pl.BlockSpec((None, Tt, D_MODEL), lambda g, t: (g, t, 0))
    s_spec = pl.BlockSpec((nseq, A_HEADS, A_DIM, A_DIM), lambda g, t: (g, 0, 0, 0))
    p_spec = pl.BlockSpec((nseq, POOL_ROWS, B_WIDTH), lambda g, t: (g, 0, 0))
    m3 = jnp.asarray(_hgrn_sum_matrix(C), BF16)
    weights = (gain, lb_logits, w_in, out_gain, pool_w, pool_scale, w_out, m3)
    w_specs = [_weight_spec(w) for w in weights]
    state_in = (s_hgrn, s_pool) if has_state else ()
    state_specs = [s_spec, p_spec] if has_state else []
    return pl.pallas_call(
        body,
        grid=(G, T // Tt),
        in_specs=[x_spec] + state_specs + w_specs,
        out_specs=(x_spec, s_spec, p_spec),
        out_shape=(jax.ShapeDtypeStruct(x.shape, F32),
                   jax.ShapeDtypeStruct((G * nseq, A_HEADS, A_DIM, A_DIM), F32),
                   jax.ShapeDtypeStruct((G * nseq, POOL_ROWS, B_WIDTH), F32)),
        scratch_shapes=[pltpu.VMEM((Tt, AB_IN), F32), pltpu.VMEM((2, Tt // 2, D_MODEL), BF16),
                        pltpu.VMEM((Tt, A_WIDTH), F32), pltpu.VMEM((Tt, A_WIDTH + B_WIDTH), BF16),
                        pltpu.VMEM((max(nseq // 2, 1) * POOL_ROWS + Tt // 2, B_WIDTH), F32),
                        pltpu.VMEM((Tt, A_WIDTH), F32), pltpu.VMEM((Tt, A_WIDTH), F32),
                        pltpu.VMEM((Tt // C, m3.shape[0], A_WIDTH), F32),
                        pltpu.VMEM((2, A_HEADS, C, C), BF16),
                        pltpu.VMEM((len(_hgrn_levels(C)) + 1, A_DIM, A_DIM), BF16),
                        pltpu.VMEM((2, A_DIM, A_DIM), BF16),
                        pltpu.VMEM((nseq, A_HEADS, A_DIM, A_DIM), F32),
                        pltpu.VMEM((nseq, POOL_ROWS, B_WIDTH), F32)],
        compiler_params=pltpu.CompilerParams(dimension_semantics=("arbitrary", "arbitrary"),
                                             vmem_limit_bytes=V7X_VMEM_LIMIT_BYTES),
        name=f"even_mixer_{layer_e}_{'s' if has_state else 'p'}",
    )(x, *state_in, *_weight_args(weights))


def _odd_body(Tt, Lc, want_v, *refs):
    if want_v:
        (x_ref, gain_ref, wuv_ref, lng_ref, lnb_ref, wmix_ref, bias_ref, wout_ref,
         xo_ref, vo_ref, zv_scr, u_scr, vn_scr, m_scr, acc_scr) = refs
    else:
        (x_ref, gain_ref, wuv_ref, lng_ref, lnb_ref, wmix_ref, bias_ref, wout_ref,
         xo_ref, zv_scr, u_scr, vn_scr, m_scr, acc_scr) = refs
    x = x_ref[...]
    h = _rms(x, gain_ref[...]).astype(BF16)
    zv_scr[...] = _dot(h, wuv_ref[:, C_WIDTH:])
    for g in range(C_GROUPS):
        u_scr[g] = _dot(h, wuv_ref[:, g * C_GROUP_DIM:(g + 1) * C_GROUP_DIM])
    zv = _gelu(zv_scr[...])
    mu = jnp.mean(zv, axis=-1, keepdims=True)
    zc = zv - mu
    vn = zc * lax.rsqrt(jnp.mean(zc * zc, axis=-1, keepdims=True) + EPS) * lng_ref[...] + lnb_ref[...]
    if want_v:
        vo_ref[...] = vn
    vn_scr[...] = vn.astype(BF16)

    def gate_rows(g):
        c0 = g * C_GROUP_DIM
        bias = bias_ref[:, g:g + 1]
        for b in range(Tt // Lc):
            m_scr[g % 2, pl.ds(b * Lc, Lc), :] = (
                _dot(wmix_ref[g], vn_scr[pl.ds(b * Lc, Lc), c0:c0 + C_GROUP_DIM]) + bias)

    gate_rows(0)
    for g in range(C_GROUPS):
        if g + 1 < C_GROUPS:
            gate_rows(g + 1)
        y = (_gelu(u_scr[g]) * m_scr[g % 2]).astype(BF16)
        part = _dot(y, wout_ref[g * C_GROUP_DIM:(g + 1) * C_GROUP_DIM, :])
        if g == 0:
            acc_scr[...] = part
        else:
            acc_scr[...] += part
    xo_ref[...] = x + acc_scr[...]


def _odd_layer(x, gain, w_uv, ln_gain, ln_bias, w_mix, bias, w_out, *, Tt, want_v):
    G, T, _ = x.shape
    Lc = w_mix.shape[-1]
    body = functools.partial(_odd_body, Tt, Lc, want_v)
    x_spec = pl.BlockSpec((None, Tt, D_MODEL), lambda g, t: (g, t, 0))
    weights = (gain, w_uv, ln_gain, ln_bias, w_mix, bias, w_out)
    out_specs = [x_spec]
    out_shape = [jax.ShapeDtypeStruct(x.shape, F32)]
    if want_v:
        out_specs.append(pl.BlockSpec((None, Tt, C_WIDTH), lambda g, t: (g, t, 0)))
        out_shape.append(jax.ShapeDtypeStruct((G, T, C_WIDTH), F32))
    return pl.pallas_call(
        body,
        grid=(G, T // Tt),
        in_specs=[x_spec] + [_weight_spec(w) for w in weights],
        out_specs=tuple(out_specs),
        out_shape=tuple(out_shape),
        scratch_shapes=[pltpu.VMEM((Tt, C_WIDTH), F32), pltpu.VMEM((C_GROUPS, Tt, C_GROUP_DIM), F32),
                        pltpu.VMEM((Tt, C_WIDTH), BF16), pltpu.VMEM((2, Tt, C_GROUP_DIM), F32),
                        pltpu.VMEM((Tt, D_MODEL), F32)],
        compiler_params=pltpu.CompilerParams(dimension_semantics=("arbitrary", "arbitrary"),
                                             vmem_limit_bytes=V7X_VMEM_LIMIT_BYTES),
        name=f"odd_mixer_{'s' if want_v else 'p'}",
    )(x, *_weight_args(weights))


def _ffn_body(nseq, L, has_state, final_norm, *refs):
    refs = list(refs)
    x_ref = refs.pop(0)
    hin_ref = refs.pop(0) if has_state else None
    gain_ref, win_ref, cw_ref, cb_ref, wd_ref = refs[:5]
    refs = refs[5:]
    gf_ref = refs.pop(0) if final_norm else None
    xo_ref, hout_ref, acc_scr, hist_scr, ag_scr, y_scr = refs
    t = pl.program_id(1)
    Tt = nseq * L

    @pl.when(t == 0)
    def _():
        if has_state:
            hist_scr[0] = hin_ref[:, 0, :]
            hist_scr[1] = hin_ref[:, 1, :]
        else:
            hist_scr[...] = jnp.zeros_like(hist_scr)

    x = x_ref[...]
    h = _rms(x, gain_ref[...]).astype(BF16)
    rowm = lax.broadcasted_iota(jnp.int32, (Tt, 1), 0) & (L - 1)

    def expand(rows):
        if nseq == 1:
            return rows
        return jnp.broadcast_to(rows[:, None, :], (nseq, L, rows.shape[-1])).reshape(Tt, rows.shape[-1])

    def up(j):
        for half, c0 in enumerate((j * FF_CHUNK, D_FF + j * FF_CHUNK)):
            ag_scr[j % 2, :, :, half * FF_CHUNK:(half + 1) * FF_CHUNK] = (
                _dot(h, win_ref[:, c0:c0 + FF_CHUNK]).reshape(nseq, L, FF_CHUNK))

    def down(j):
        part = _dot(y_scr[j % 2], wd_ref[j * FF_CHUNK:(j + 1) * FF_CHUNK, :])
        if j == 0:
            acc_scr[...] = part
        else:
            acc_scr[...] += part

    up(0)
    for j in range(N_FF_CHUNKS):
        cols = slice(j * FF_CHUNK, (j + 1) * FF_CHUNK)
        slot = j % 2
        if j + 1 < N_FF_CHUNKS:
            up(j + 1)
        a = ag_scr[slot, :, :, :FF_CHUNK].reshape(Tt, FF_CHUNK)
        gate = ag_scr[slot, :, :, FF_CHUNK:].reshape(Tt, FF_CHUNK)
        h2 = expand(hist_scr[0, :, cols])
        h1 = expand(hist_scr[1, :, cols])
        a1 = jnp.where(rowm >= 1, pltpu.roll(a, 1, 0), h1)
        a2 = jnp.where(rowm >= 2, pltpu.roll(a, 2, 0), jnp.where(rowm == 1, h1, h2))
        conv = (cb_ref[:, cols] + a2 * cw_ref[pl.ds(0, 1), cols] + a1 * cw_ref[pl.ds(1, 1), cols]
                + a * cw_ref[pl.ds(2, 1), cols])
        y_scr[slot] = (_gelu(conv) * gate).astype(BF16)
        hist_scr[0, :, cols] = ag_scr[slot, :, L - 2, :FF_CHUNK]
        hist_scr[1, :, cols] = ag_scr[slot, :, L - 1, :FF_CHUNK]
        if j > 0:
            down(j - 1)
    down(N_FF_CHUNKS - 1)
    xn = x + acc_scr[...]
    if final_norm:
        xn = _rms(xn, gf_ref[...])
    xo_ref[...] = xn
    hout_ref[:, 0, :] = hist_scr[0]
    hout_ref[:, 1, :] = hist_scr[1]


def _ffn_layer(x, s_conv, gain, w_in, conv_w, conv_b, w_d, final_gain, *, nseq, L):
    G, T, _ = x.shape
    Tt = nseq * L
    has_state = s_conv is not None
    final_norm = final_gain is not None
    body = functools.partial(_ffn_body, nseq, L, has_state, final_norm)
    x_spec = pl.BlockSpec((None, Tt, D_MODEL), lambda g, t: (g, t, 0))
    h_spec = pl.BlockSpec((nseq, CONV_W - 1, D_FF), lambda g, t: (g, 0, 0))
    weights = (gain, w_in, conv_w, conv_b, w_d) + ((final_gain,) if final_norm else ())
    state_in = (s_conv,) if has_state else ()
    return pl.pallas_call(
        body,
        grid=(G, T // Tt),
        in_specs=[x_spec] + ([h_spec] if has_state else []) + [_weight_spec(w) for w in weights],
        out_specs=(x_spec, h_spec),
        out_shape=(jax.ShapeDtypeStruct(x.shape, F32),
                   jax.ShapeDtypeStruct((G * nseq, CONV_W - 1, D_FF), F32)),
        scratch_shapes=[pltpu.VMEM((Tt, D_MODEL), F32), pltpu.VMEM((CONV_W - 1, nseq, D_FF), F32),
                        pltpu.VMEM((2, nseq, L, 2 * FF_CHUNK), F32),
                        pltpu.VMEM((2, Tt, FF_CHUNK), BF16)],
        compiler_params=pltpu.CompilerParams(dimension_semantics=("arbitrary", "arbitrary"),
                                             vmem_limit_bytes=V7X_VMEM_LIMIT_BYTES),
        name=f"conv_ffn_{'s' if has_state else 'p'}{'_final' if final_norm else ''}",
    )(x, *state_in, *_weight_args(weights))


def _trunk(x, s_hgrn, s_pool, s_conv, pos0, w, *, nseq, L, want_v):
    depth = w["norm_mix"].shape[0]
    n_even = w["lb_logits"].shape[0]
    Tt = nseq * L
    C = min(HGRN_CHUNK, L)
    new_hgrn, new_pool, new_conv, new_cv = [], [], [], []
    for layer in range(depth):
        if layer % 2 == 0:
            e = layer // 2
            x, sh, sp = _even_layer(
                e, n_even, x, None if s_hgrn is None else s_hgrn[e], None if s_pool is None else s_pool[e],
                w["norm_mix"][layer][None], w["lb_logits"], _Layer(w["w_in_ab"], e), w["hgrn_out_gain"][e][None],
                _Layer(w["pool_w"], e), w["pool_scale"][e][None], _Layer(w["w_out_ab"], e),
                nseq=nseq, L=L, C=C, pos0=pos0)
            new_hgrn.append(sh)
            new_pool.append(sp)
        else:
            o = layer // 2
            out = _odd_layer(x, w["norm_mix"][layer][None], _Layer(w["w_uv"], o), w["c_ln_gain"][o][None],
                             w["c_ln_bias"][o][None], _Layer(w["w_mix"], o), _Layer(w["c_bias"], o),
                             _Layer(w["w_out_c"], o),
                             Tt=Tt, want_v=want_v)
            x = out[0]
            if want_v:
                new_cv.append(out[1])
        final_gain = w["norm_final"][None] if layer == depth - 1 else None
        x, sc = _ffn_layer(x, None if s_conv is None else s_conv[layer], w["norm_ffn"][layer][None],
                           _Layer(w["w_in_ffn"], layer), _Layer(w["conv_w"], layer), _Layer(w["conv_b"], layer),
                           _Layer(w["w_d"], layer), final_gain, nseq=nseq, L=L)
        new_conv.append(sc)
    return x, jnp.stack(new_hgrn), jnp.stack(new_pool), jnp.stack(new_conv), new_cv


def kernel(x_prompt, x_sample, state_hgrn, state_pool, state_ffn_conv, norm_mix, norm_ffn, norm_final, hgrn_lb_logits, w_in_ab, hgrn_out_gain, pool_w, pool_scale, w_out_ab, w_uv_c, c_ln_gain, c_ln_bias, c_ws, c_bs, w_out_c, w_in_ffn, ffn_conv_w, ffn_conv_b, w_out_ffn):
    depth = norm_mix.shape[0]
    n_odd = w_uv_c.shape[0]
    bp, seq, _ = x_prompt.shape
    bs, dseq, _ = x_sample.shape

    common = dict(
        norm_mix=norm_mix, norm_ffn=norm_ffn, norm_final=norm_final, lb_logits=hgrn_lb_logits,
        w_in_ab=w_in_ab.astype(BF16), hgrn_out_gain=hgrn_out_gain, pool_w=pool_w.astype(BF16),
        pool_scale=pool_scale, w_out_ab=w_out_ab.astype(BF16),
        w_uv=w_uv_c.astype(BF16), c_ln_gain=c_ln_gain, c_ln_bias=c_ln_bias, w_out_c=w_out_c.astype(BF16),
        w_in_ffn=w_in_ffn.astype(BF16), conv_w=ffn_conv_w, conv_b=ffn_conv_b[:, None, :],
        w_d=w_out_ffn.astype(BF16),
    )

    def gating(length, copies):
        idx = jnp.arange(length)
        mask = (idx[None, :] // C_CAUSAL) <= (idx[:, None] // C_CAUSAL)
        wm = jnp.where(mask[None, None], c_ws[:, :, :length, :length], 0.0)
        if copies > 1:
            eye = jnp.eye(copies, dtype=wm.dtype)
            wm = jnp.einsum("ab,ogts->ogatbs", eye, wm).reshape(n_odd, C_GROUPS, copies * length, copies * length)
        bias = jnp.tile(jnp.swapaxes(c_bs[:, :, :length], 1, 2), (1, copies, 1))
        return wm.astype(BF16), bias

    wm_p, bias_p = gating(min(C_BLOCK, seq), 1)
    Lp = min(PROMPT_TILE, seq)
    y_p, hg_p, pool_p, conv_p, _ = _trunk(x_prompt, None, None, None, 0, dict(common, w_mix=wm_p, c_bias=bias_p),
                                          nseq=1, L=Lp, want_v=False)

    wm_s, bias_s = gating(min(C_BLOCK, dseq), bs)
    pool_hist = jnp.pad(state_pool, ((0, 0), (0, 0), (POOL_ROWS - state_pool.shape[2], 0), (0, 0)))
    y_s, hg_s, pool_s, conv_s, cv_s = _trunk(
        x_sample.reshape(1, bs * dseq, D_MODEL), state_hgrn, pool_hist, state_ffn_conv, PAST_LEN,
        dict(common, w_mix=wm_s, c_bias=bias_s), nseq=bs, L=dseq, want_v=True)
    n_hist = state_pool.shape[2]
    cv_s = jnp.stack(cv_s).reshape(n_odd, bs, dseq, C_WIDTH)
    return (y_p, y_s.reshape(bs, dseq, D_MODEL), hg_p, hg_s, pool_p[:, :, POOL_ROWS - n_hist:],
            pool_s[:, :, POOL_ROWS - n_hist:], conv_p, conv_s, cv_s)
```

```python
import functools
from typing import NamedTuple

import numpy as np
import jax
import jax.numpy as jnp
from jax import lax
from jax.experimental import pallas as pl
from jax.experimental.pallas import tpu as pltpu

F32 = jnp.float32
BF16 = jnp.bfloat16

D_MODEL = 1024
EPS = 1e-6
PAST_LEN = 4096
SUBLANES = 8
A_HEADS = 4
A_DIM = 128
A_WIDTH = A_HEADS * A_DIM
HGRN_CHUNK = 64
LOG2_E = 1.4426950408889634
POOL_WINDOWS = (2, 4, 8, 16)
B_GROUP_DIM = 128
B_WIDTH = len(POOL_WINDOWS) * B_GROUP_DIM
POOL_ROWS = 16
AB_IN = 4 * A_WIDTH + B_WIDTH
IN_BLOCK = 256
OUT_BLOCK = 256
C_BLOCK = 128
C_CAUSAL = 64
C_GROUPS = 8
C_WIDTH = 2 * D_MODEL
C_GROUP_DIM = C_WIDTH // C_GROUPS
D_FF = 2816
FF_CHUNK = 256
N_FF_CHUNKS = D_FF // FF_CHUNK
CONV_W = 3

PROMPT_TILE = 512
V7X_VMEM_LIMIT_BYTES = 56 * 1024 * 1024


def _dot(a, b):
    return jnp.dot(a, b, preferred_element_type=F32)


def _dot_nt(a, b):
    return lax.dot_general(a, b, (((1,), (1,)), ((), ())), preferred_element_type=F32)


def _dot_tn(a, b):
    return lax.dot_general(a, b, (((0,), (0,)), ((), ())), preferred_element_type=F32)


def _rms(x, gain):
    return x * lax.rsqrt(jnp.mean(x * x, axis=-1, keepdims=True) + EPS) * gain


def _sigmoid(x):
    return 1.0 / (1.0 + jnp.exp(-x))


def _gelu(x):
    return 0.5 * x * (1.0 + lax.erf(x * 0.7071067811865476))


def _hgrn_levels(C):
    levels, size = [], 2
    while size <= C:
        levels.append(size)
        size *= 2
    return levels


def _hgrn_sum_matrix(C):
    t = np.arange(C)[:, None]
    s = np.arange(C)[None, :]
    blocks = [s <= t, s > t]
    for size in _hgrn_levels(C):
        half = size // 2
        same = (t // size) == (s // size)
        blocks.append((same & (t % size >= half) & (s % size >= half) & (s <= t))
                      | (same & (t % size < half) & (s % size < half) & (s > t)))
    blocks.append(np.ones((SUBLANES, C), bool))
    m = np.concatenate(blocks, axis=0).astype(np.float32)
    return np.concatenate([m, m, m], axis=1)


def _even_body(layer_e, n_even, nseq, L, C, pos0, has_state, *refs):
    if has_state:
        (x_ref, sin_ref, pin_ref, gain_ref, lbl_ref, win_ref, og_ref, pw_ref, ps_ref, wout_ref, m3_ref,
         xo_ref, sout_ref, pout_ref, proj_scr, h_scr, o_scr, cat_scr, ext_scr, q_scr, k_scr, e_scr, p_scr, s_scr, ph_scr) = refs
    else:
        (x_ref, gain_ref, lbl_ref, win_ref, og_ref, pw_ref, ps_ref, wout_ref, m3_ref,
         xo_ref, sout_ref, pout_ref, proj_scr, h_scr, o_scr, cat_scr, ext_scr, q_scr, k_scr, e_scr, p_scr, s_scr, ph_scr) = refs
    t = pl.program_id(1)
    Tt = nseq * L
    levels = _hgrn_levels(C)
    e_rows = e_scr.shape[1]

    @pl.when(t == 0)
    def _():
        if has_state:
            for s in range(nseq):
                for hh in range(A_HEADS):
                    s_scr[s, hh] = sin_ref[s, hh].T
            ph_scr[...] = pin_ref[...]
        else:
            s_scr[...] = jnp.zeros_like(s_scr)
            ph_scr[...] = jnp.zeros_like(ph_scr)

    lrows = [lbl_ref[pl.ds(i, 1), :] for i in range(n_even)]
    lmax = functools.reduce(jnp.maximum, lrows)
    lexp = [jnp.exp(r - lmax) for r in lrows]
    lden = functools.reduce(lambda a, b: a + b, lexp)
    lb = jnp.zeros_like(lden)
    for i in range(1, layer_e + 1):
        lb = lb + lexp[i] / lden
    log_lb = jnp.log(lb)
    log1m_lb = jnp.log1p(-lb)
    one_m_lb = 1.0 - lb

    qi = lax.broadcasted_iota(jnp.int32, (C, C), 0)
    ki = lax.broadcasted_iota(jnp.int32, (C, C), 1)
    diagonal = qi == ki
    owned = {size: ((qi // size) == (ki // size)) & ((qi & (size - 1)) >= size // 2) & ((ki & (size - 1)) < size // 2)
             for size in levels}
    chunks_per_seg = L // C

    n_chunks = Tt // C
    heads = [slice(hh * A_DIM, (hh + 1) * A_DIM) for hh in range(A_HEADS)]

    Hn = Tt // 2
    seq_h = max(nseq // 2, 1)
    L_h = Hn // seq_h
    seg_rows = POOL_ROWS + L_h
    chunks_h = n_chunks // 2

    def project_pieces(hf):
        rows = slice(hf * Hn, (hf + 1) * Hn)

        def norm():
            h_scr[hf] = _rms(x_ref[rows, :], gain_ref[...]).astype(BF16)

        def block(cb):
            cols = slice(cb * IN_BLOCK, (cb + 1) * IN_BLOCK)
            proj_scr[rows, cols] = _dot(h_scr[hf], win_ref[:, cols])

        def gates():
            qr = proj_scr[rows, 0:A_WIDTH]
            fz = proj_scr[rows, A_WIDTH:2 * A_WIDTH]
            q_scr[rows, :] = qr * _sigmoid(qr) * (A_DIM ** -0.5)
            log_sig = jnp.minimum(fz, 0.0) - jnp.log(1.0 + jnp.exp(-jnp.abs(fz)))
            y = log1m_lb + log_sig
            lf = jnp.maximum(log_lb, y) + jnp.log(1.0 + jnp.exp(-jnp.abs(log_lb - y)))
            k_scr[rows, :] = one_m_lb * _sigmoid(-fz)
            lf2 = lf * LOG2_E
            hi = lf2.astype(BF16)
            r1 = lf2 - hi.astype(F32)
            mid = r1.astype(BF16)
            lo = (r1 - mid.astype(F32)).astype(BF16)
            for ci in range(chunks_h):
                rs = slice(ci * C, (ci + 1) * C)
                e_scr[hf * chunks_h + ci] = _dot(m3_ref[...], jnp.concatenate([hi[rs], mid[rs], lo[rs]], axis=0))

        return [norm] + [functools.partial(block, cb) for cb in range(AB_IN // IN_BLOCK)] + [gates]

    def scores(c, hh):
        rs, cs = slice(c * C, (c + 1) * C), heads[hh]
        q_b, k_b = q_scr[rs, cs].astype(BF16), k_scr[rs, cs].astype(BF16)
        p = jnp.where(diagonal, _dot_nt(q_b, k_b), 0.0)
        for li, size in enumerate(levels):
            r0 = (2 + li) * C
            decay = jnp.exp2(e_scr[c, r0:r0 + C, cs]).astype(BF16)
            p = jnp.where(owned[size], _dot_nt(q_b * decay, k_b * decay), p)
        p_scr[c % 2, hh] = p.astype(BF16)

    def carried(c, hh):
        rs, cs = slice(c * C, (c + 1) * C), heads[hh]
        seg = c // chunks_per_seg
        state_t = s_scr[seg, hh]
        qs = (q_scr[rs, cs] * jnp.exp2(e_scr[c, 0:C, cs])).astype(BF16)
        ks = (k_scr[rs, cs] * jnp.exp2(e_scr[c, C:2 * C, cs])).astype(BF16)
        o_scr[rs, cs] = _dot_nt(qs, state_t.astype(BF16))
        dec = jnp.exp2(e_scr[c, e_rows - SUBLANES:e_rows - SUBLANES + 1, cs])
        s_scr[seg, hh] = state_t * dec + _dot_tn(proj_scr[rs, 2 * A_WIDTH + hh * A_DIM:
                                                          2 * A_WIDTH + (hh + 1) * A_DIM].astype(BF16), ks)

    def weighted(c, hh):
        rs, cs = slice(c * C, (c + 1) * C), heads[hh]
        v = proj_scr[rs, 2 * A_WIDTH + hh * A_DIM:2 * A_WIDTH + (hh + 1) * A_DIM].astype(BF16)
        o_scr[rs, cs] += _dot(p_scr[c % 2, hh], v)

    def recur_pieces(hf, ahead=True):
        pieces = []
        for c in range(hf * chunks_h, (hf + 1) * chunks_h):
            last = c + 1 == (hf + 1) * chunks_h
            pieces += [functools.partial(carried, c, hh) for hh in range(A_HEADS)]
            if c + 1 < n_chunks and (ahead or not last):
                pieces += [functools.partial(scores, c + 1, hh) for hh in range(A_HEADS)]
            pieces += [functools.partial(weighted, c, hh) for hh in range(A_HEADS)]
        return pieces

    def emit_pieces(hf):
        rows = slice(hf * Hn, (hf + 1) * Hn)
        seq0 = hf * seq_h if nseq > 1 else 0

        def history():
            for s in range(seq_h):
                ext_scr[pl.ds(s * seg_rows, POOL_ROWS), :] = ph_scr[seq0 + s]
                ext_scr[pl.ds(s * seg_rows + POOL_ROWS, L_h), :] = proj_scr[pl.ds(hf * Hn + s * L_h, L_h),
                                                                            4 * A_WIDTH:AB_IN]
            for s in range(seq_h):
                ph_scr[seq0 + s] = ext_scr[pl.ds(s * seg_rows + L_h, POOL_ROWS), :]

        def new_rows(a):
            if seq_h == 1:
                return a[POOL_ROWS:]
            return a.reshape(seq_h, seg_rows, a.shape[-1])[:, POOL_ROWS:, :].reshape(Hn, a.shape[-1])

        def pool(gi):
            w = POOL_WINDOWS[gi]
            c0 = gi * B_GROUP_DIM
            first_pos = pos0 + t * L + (hf * L_h if nseq == 1 else 0)
            pos = first_pos + (lax.broadcasted_iota(jnp.int32, (Hn, 1), 0) & (L_h - 1))
            cur = ext_scr[:, c0:c0 + B_GROUP_DIM]
            acc = cur
            sh = 1
            while sh < w:
                acc = acc + pltpu.roll(acc, sh, 0)
                sh *= 2
            cnt = jnp.minimum(pos + 1, w).astype(F32)
            pooled = new_rows(acc) / cnt - new_rows(cur)
            yb = _dot(pooled.astype(BF16), pw_ref[gi]) * ps_ref[:, c0:c0 + B_GROUP_DIM]
            cat_scr[rows, A_WIDTH + c0:A_WIDTH + c0 + B_GROUP_DIM] = yb.astype(BF16)

        def gate(hh):
            c0 = hh * A_DIM
            oh = o_scr[rows, c0:c0 + A_DIM]
            g = proj_scr[rows, 3 * A_WIDTH + c0:3 * A_WIDTH + c0 + A_DIM]
            cat_scr[rows, c0:c0 + A_DIM] = (_rms(oh, og_ref[...]) * (g * _sigmoid(g))).astype(BF16)

        def block(cb):
            cols = slice(cb * OUT_BLOCK, (cb + 1) * OUT_BLOCK)
            xo_ref[rows, cols] = x_ref[rows, cols] + _dot(cat_scr[rows, :], wout_ref[:, cols])

        return ([history] + [functools.partial(pool, gi) for gi in range(len(POOL_WINDOWS))]
                + [functools.partial(gate, hh) for hh in range(A_HEADS)]
                + [functools.partial(block, cb) for cb in range(D_MODEL // OUT_BLOCK)])

    def interleave(main, side):
        done = 0
        for i, piece in enumerate(main):
            due = (i + 1) * len(side) // len(main)
            for extra in side[done:due]:
                extra()
            done = due
            piece()

    def run(pieces):
        for piece in pieces:
            piece()

    run(project_pieces(0))
    run([functools.partial(scores, 0, hh) for hh in range(A_HEADS)])
    interleave(recur_pieces(0, ahead=False), project_pieces(1))
    run([functools.partial(scores, chunks_h, hh) for hh in range(A_HEADS)])
    interleave(recur_pieces(1), emit_pieces(0))
    run(emit_pieces(1))

    for s in range(nseq):
        for hh in range(A_HEADS):
            sout_ref[s, hh] = s_scr[s, hh].T
    pout_ref[...] = ph_scr[...]


class _Layer(NamedTuple):
    stacked: jax.Array
    index: int

    @property
    def shape(self):
        return self.stacked.shape[1:]


def _weight_spec(w):
    if isinstance(w, _Layer):
        index = (w.index,) + (0,) * len(w.shape)
        return pl.BlockSpec((None,) + tuple(w.shape), lambda g, t: index, pipeline_mode=pl.Buffered(1))
    zeros = (0,) * w.ndim
    return pl.BlockSpec(w.shape, lambda g, t: zeros, pipeline_mode=pl.Buffered(1))


def _weight_args(weights):
    return [w.stacked if isinstance(w, _Layer) else w for w in weights]


def _even_layer(layer_e, n_even, x, s_hgrn, s_pool, gain, lb_logits, w_in, out_gain, pool_w, pool_scale, w_out,
                *, nseq, L, C, pos0):
    G, T, _ = x.shape
    Tt = nseq * L
    has_state = s_hgrn is not None
    body = functools.partial(_even_body, layer_e, n_even, nseq, L, C, pos0, has_state)
    x_spec = pl.BlockSpec((None, Tt, D_MODEL), lambda g, t: (g, t, 0))
    s_spec = pl.BlockSpec((nseq, A_HEADS, A_DIM, A_DIM), lambda g, t: (g, 0, 0, 0))
    p_spec = pl.BlockSpec((nseq, POOL_ROWS, B_WIDTH), lambda g, t: (g, 0, 0))
    m3 = jnp.asarray(_hgrn_sum_matrix(C), BF16)
    weights = (gain, lb_logits, w_in, out_gain, pool_w, pool_scale, w_out, m3)
    w_specs = [_weight_spec(w) for w in weights]
    state_in = (s_hgrn, s_pool) if has_state else ()
    state_specs = [s_spec, p_spec] if has_state else []
    return pl.pallas_call(
        body,
        grid=(G, T // Tt),
        in_specs=[x_spec] + state_specs + w_specs,
        out_specs=(x_spec, s_spec, p_spec),
        out_shape=(jax.ShapeDtypeStruct(x.shape, F32),
                   jax.ShapeDtypeStruct((G * nseq, A_HEADS, A_DIM, A_DIM), F32),
                   jax.ShapeDtypeStruct((G * nseq, POOL_ROWS, B_WIDTH), F32)),
        scratch_shapes=[pltpu.VMEM((Tt, AB_IN), F32), pltpu.VMEM((2, Tt // 2, D_MODEL), BF16),
                        pltpu.VMEM((Tt, A_WIDTH), F32), pltpu.VMEM((Tt, A_WIDTH + B_WIDTH), BF16),
                        pltpu.VMEM((max(nseq // 2, 1) * POOL_ROWS + Tt // 2, B_WIDTH), F32),
                        pltpu.VMEM((Tt, A_WIDTH), F32), pltpu.VMEM((Tt, A_WIDTH), F32),
                        pltpu.VMEM((Tt // C, m3.shape[0], A_WIDTH), F32),
                        pltpu.VMEM((2, A_HEADS, C, C), BF16),
                        pltpu.VMEM((nseq, A_HEADS, A_DIM, A_DIM), F32),
                        pltpu.VMEM((nseq, POOL_ROWS, B_WIDTH), F32)],
        compiler_params=pltpu.CompilerParams(dimension_semantics=("arbitrary", "arbitrary"),
                                             vmem_limit_bytes=V7X_VMEM_LIMIT_BYTES),
        name=f"even_mixer_{layer_e}_{'s' if has_state else 'p'}",
    )(x, *state_in, *_weight_args(weights))


def _odd_body(Tt, Lc, want_v, *refs):
    if want_v:
        (x_ref, gain_ref, wuv_ref, lng_ref, lnb_ref, wmix_ref, bias_ref, wout_ref,
         xo_ref, vo_ref, zv_scr, u_scr, vn_scr, m_scr, acc_scr) = refs
    else:
        (x_ref, gain_ref, wuv_ref, lng_ref, lnb_ref, wmix_ref, bias_ref, wout_ref,
         xo_ref, zv_scr, u_scr, vn_scr, m_scr, acc_scr) = refs
    x = x_ref[...]
    h = _rms(x, gain_ref[...]).astype(BF16)
    zv_scr[...] = _dot(h, wuv_ref[:, C_WIDTH:])
    for g in range(C_GROUPS):
        u_scr[g] = _gelu(_dot(h, wuv_ref[:, g * C_GROUP_DIM:(g + 1) * C_GROUP_DIM])).astype(BF16)
    zv = _gelu(zv_scr[...])
    mu = jnp.mean(zv, axis=-1, keepdims=True)
    zc = zv - mu
    vn = zc * lax.rsqrt(jnp.mean(zc * zc, axis=-1, keepdims=True) + EPS) * lng_ref[...] + lnb_ref[...]
    if want_v:
        vo_ref[...] = vn
    vn_scr[...] = vn.astype(BF16)

    def gate_rows(g):
        c0 = g * C_GROUP_DIM
        bias = bias_ref[:, g:g + 1]
        for b in range(Tt // Lc):
            m_scr[g % 2, pl.ds(b * Lc, Lc), :] = (
                _dot(wmix_ref[g], vn_scr[pl.ds(b * Lc, Lc), c0:c0 + C_GROUP_DIM]) + bias)

    gate_rows(0)
    for g in range(C_GROUPS):
        if g + 1 < C_GROUPS:
            gate_rows(g + 1)
        y = (u_scr[g] * m_scr[g % 2]).astype(BF16)
        part = _dot(y, wout_ref[g * C_GROUP_DIM:(g + 1) * C_GROUP_DIM, :])
        if g == 0:
            acc_scr[...] = part
        else:
            acc_scr[...] += part
    xo_ref[...] = x + acc_scr[...]


def _odd_layer(x, gain, w_uv, ln_gain, ln_bias, w_mix, bias, w_out, *, Tt, want_v):
    G, T, _ = x.shape
    Lc = w_mix.shape[-1]
    body = functools.partial(_odd_body, Tt, Lc, want_v)
    x_spec = pl.BlockSpec((None, Tt, D_MODEL), lambda g, t: (g, t, 0))
    weights = (gain, w_uv, ln_gain, ln_bias, w_mix, bias, w_out)
    out_specs = [x_spec]
    out_shape = [jax.ShapeDtypeStruct(x.shape, F32)]
    if want_v:
        out_specs.append(pl.BlockSpec((None, Tt, C_WIDTH), lambda g, t: (g, t, 0)))
        out_shape.append(jax.ShapeDtypeStruct((G, T, C_WIDTH), F32))
    return pl.pallas_call(
        body,
        grid=(G, T // Tt),
        in_specs=[x_spec] + [_weight_spec(w) for w in weights],
        out_specs=tuple(out_specs),
        out_shape=tuple(out_shape),
        scratch_shapes=[pltpu.VMEM((Tt, C_WIDTH), F32), pltpu.VMEM((C_GROUPS, Tt, C_GROUP_DIM), BF16),
                        pltpu.VMEM((Tt, C_WIDTH), BF16), pltpu.VMEM((2, Tt, C_GROUP_DIM), F32),
                        pltpu.VMEM((Tt, D_MODEL), F32)],
        compiler_params=pltpu.CompilerParams(dimension_semantics=("arbitrary", "arbitrary"),
                                             vmem_limit_bytes=V7X_VMEM_LIMIT_BYTES),
        name=f"odd_mixer_{'s' if want_v else 'p'}",
    )(x, *_weight_args(weights))


def _ffn_body(nseq, L, has_state, final_norm, *refs):
    refs = list(refs)
    x_ref = refs.pop(0)
    hin_ref = refs.pop(0) if has_state else None
    gain_ref, win_ref, cw_ref, cb_ref, wd_ref = refs[:5]
    refs = refs[5:]
    gf_ref = refs.pop(0) if final_norm else None
    xo_ref, hout_ref, acc_scr, hist_scr, ag_scr, y_scr = refs
    t = pl.program_id(1)
    Tt = nseq * L

    @pl.when(t == 0)
    def _():
        if has_state:
            hist_scr[0] = hin_ref[:, 0, :]
            hist_scr[1] = hin_ref[:, 1, :]
        else:
            hist_scr[...] = jnp.zeros_like(hist_scr)

    x = x_ref[...]
    h = _rms(x, gain_ref[...]).astype(BF16)
    rowm = lax.broadcasted_iota(jnp.int32, (Tt, 1), 0) & (L - 1)

    def expand(rows):
        if nseq == 1:
            return rows
        return jnp.broadcast_to(rows[:, None, :], (nseq, L, rows.shape[-1])).reshape(Tt, rows.shape[-1])

    def up(j):
        for half, c0 in enumerate((j * FF_CHUNK, D_FF + j * FF_CHUNK)):
            ag_scr[j % 2, :, :, half * FF_CHUNK:(half + 1) * FF_CHUNK] = (
                _dot(h, win_ref[:, c0:c0 + FF_CHUNK]).reshape(nseq, L, FF_CHUNK))

    def down(j):
        part = _dot(y_scr[j % 2], wd_ref[j * FF_CHUNK:(j + 1) * FF_CHUNK, :])
        if j == 0:
            acc_scr[...] = part
        else:
            acc_scr[...] += part

    up(0)
    for j in range(N_FF_CHUNKS):
        cols = slice(j * FF_CHUNK, (j + 1) * FF_CHUNK)
        slot = j % 2
        if j + 1 < N_FF_CHUNKS:
            up(j + 1)
        a = ag_scr[slot, :, :, :FF_CHUNK].reshape(Tt, FF_CHUNK)
        gate = ag_scr[slot, :, :, FF_CHUNK:].reshape(Tt, FF_CHUNK)
        h2 = expand(hist_scr[0, :, cols])
        h1 = expand(hist_scr[1, :, cols])
        a1 = jnp.where(rowm >= 1, pltpu.roll(a, 1, 0), h1)
        a2 = jnp.where(rowm >= 2, pltpu.roll(a, 2, 0), jnp.where(rowm == 1, h1, h2))
        conv = (cb_ref[:, cols] + a2 * cw_ref[pl.ds(0, 1), cols] + a1 * cw_ref[pl.ds(1, 1), cols]
                + a * cw_ref[pl.ds(2, 1), cols])
        y_scr[slot] = (_gelu(conv) * gate).astype(BF16)
        hist_scr[0, :, cols] = ag_scr[slot, :, L - 2, :FF_CHUNK]
        hist_scr[1, :, cols] = ag_scr[slot, :, L - 1, :FF_CHUNK]
        if j > 0:
            down(j - 1)
    down(N_FF_CHUNKS - 1)
    xn = x + acc_scr[...]
    if final_norm:
        xn = _rms(xn, gf_ref[...])
    xo_ref[...] = xn
    hout_ref[:, 0, :] = hist_scr[0]
    hout_ref[:, 1, :] = hist_scr[1]


def _ffn_layer(x, s_conv, gain, w_in, conv_w, conv_b, w_d, final_gain, *, nseq, L):
    G, T, _ = x.shape
    Tt = nseq * L
    has_state = s_conv is not None
    final_norm = final_gain is not None
    body = functools.partial(_ffn_body, nseq, L, has_state, final_norm)
    x_spec = pl.BlockSpec((None, Tt, D_MODEL), lambda g, t: (g, t, 0))
    h_spec = pl.BlockSpec((nseq, CONV_W - 1, D_FF), lambda g, t: (g, 0, 0))
    weights = (gain, w_in, conv_w, conv_b, w_d) + ((final_gain,) if final_norm else ())
    state_in = (s_conv,) if has_state else ()
    return pl.pallas_call(
        body,
        grid=(G, T // Tt),
        in_specs=[x_spec] + ([h_spec] if has_state else []) + [_weight_spec(w) for w in weights],
        out_specs=(x_spec, h_spec),
        out_shape=(jax.ShapeDtypeStruct(x.shape, F32),
                   jax.ShapeDtypeStruct((G * nseq, CONV_W - 1, D_FF), F32)),
        scratch_shapes=[pltpu.VMEM((Tt, D_MODEL), F32), pltpu.VMEM((CONV_W - 1, nseq, D_FF), F32),
                        pltpu.VMEM((2, nseq, L, 2 * FF_CHUNK), F32),
                        pltpu.VMEM((2, Tt, FF_CHUNK), BF16)],
        compiler_params=pltpu.CompilerParams(dimension_semantics=("arbitrary", "arbitrary"),
                                             vmem_limit_bytes=V7X_VMEM_LIMIT_BYTES),
        name=f"conv_ffn_{'s' if has_state else 'p'}{'_final' if final_norm else ''}",
    )(x, *state_in, *_weight_args(weights))


def _trunk(x, s_hgrn, s_pool, s_conv, pos0, w, *, nseq, L, want_v):
    depth = w["norm_mix"].shape[0]
    n_even = w["lb_logits"].shape[0]
    Tt = nseq * L
    C = min(HGRN_CHUNK, L)
    new_hgrn, new_pool, new_conv, new_cv = [], [], [], []
    for layer in range(depth):
        if layer % 2 == 0:
            e = layer // 2
            x, sh, sp = _even_layer(
                e, n_even, x, None if s_hgrn is None else s_hgrn[e], None if s_pool is None else s_pool[e],
                w["norm_mix"][layer][None], w["lb_logits"], _Layer(w["w_in_ab"], e), w["hgrn_out_gain"][e][None],
                _Layer(w["pool_w"], e), w["pool_scale"][e][None], _Layer(w["w_out_ab"], e),
                nseq=nseq, L=L, C=C, pos0=pos0)
            new_hgrn.append(sh)
            new_pool.append(sp)
        else:
            o = layer // 2
            out = _odd_layer(x, w["norm_mix"][layer][None], _Layer(w["w_uv"], o), w["c_ln_gain"][o][None],
                             w["c_ln_bias"][o][None], _Layer(w["w_mix"], o), _Layer(w["c_bias"], o),
                             _Layer(w["w_out_c"], o),
                             Tt=Tt, want_v=want_v)
            x = out[0]
            if want_v:
                new_cv.append(out[1])
        final_gain = w["norm_final"][None] if layer == depth - 1 else None
        x, sc = _ffn_layer(x, None if s_conv is None else s_conv[layer], w["norm_ffn"][layer][None],
                           _Layer(w["w_in_ffn"], layer), _Layer(w["conv_w"], layer), _Layer(w["conv_b"], layer),
                           _Layer(w["w_d"], layer), final_gain, nseq=nseq, L=L)
        new_conv.append(sc)
    return x, jnp.stack(new_hgrn), jnp.stack(new_pool), jnp.stack(new_conv), new_cv


def kernel(x_prompt, x_sample, state_hgrn, state_pool, state_ffn_conv, norm_mix, norm_ffn, norm_final, hgrn_lb_logits, w_in_ab, hgrn_out_gain, pool_w, pool_scale, w_out_ab, w_uv_c, c_ln_gain, c_ln_bias, c_ws, c_bs, w_out_c, w_in_ffn, ffn_conv_w, ffn_conv_b, w_out_ffn):
    depth = norm_mix.shape[0]
    n_odd = w_uv_c.shape[0]
    bp, seq, _ = x_prompt.shape
    bs, dseq, _ = x_sample.shape

    common = dict(
        norm_mix=norm_mix, norm_ffn=norm_ffn, norm_final=norm_final, lb_logits=hgrn_lb_logits,
        w_in_ab=w_in_ab.astype(BF16), hgrn_out_gain=hgrn_out_gain, pool_w=pool_w.astype(BF16),
        pool_scale=pool_scale, w_out_ab=w_out_ab.astype(BF16),
        w_uv=w_uv_c.astype(BF16), c_ln_gain=c_ln_gain, c_ln_bias=c_ln_bias, w_out_c=w_out_c.astype(BF16),
        w_in_ffn=w_in_ffn.astype(BF16), conv_w=ffn_conv_w, conv_b=ffn_conv_b[:, None, :],
        w_d=w_out_ffn.astype(BF16),
    )

    def gating(length, copies):
        idx = jnp.arange(length)
        mask = (idx[None, :] // C_CAUSAL) <= (idx[:, None] // C_CAUSAL)
        wm = jnp.where(mask[None, None], c_ws[:, :, :length, :length], 0.0)
        wm = wm.astype(BF16)
        if copies > 1:
            r = jnp.arange(copies * length)
            sel = (r[:, None] % length == idx[None, :]).astype(BF16)
            tiled = jnp.einsum("ri,ogij,cj->ogrc", sel, wm, sel, preferred_element_type=F32)
            wm = jnp.where((r[:, None] // length) == (r[None, :] // length), tiled, 0.0).astype(BF16)
        bias = jnp.tile(jnp.swapaxes(c_bs[:, :, :length], 1, 2), (1, copies, 1))
        return wm, bias

    wm_p, bias_p = gating(min(C_BLOCK, seq), 1)
    Lp = min(PROMPT_TILE, seq)
    y_p, hg_p, pool_p, conv_p, _ = _trunk(x_prompt, None, None, None, 0, dict(common, w_mix=wm_p, c_bias=bias_p),
                                          nseq=1, L=Lp, want_v=False)

    wm_s, bias_s = gating(min(C_BLOCK, dseq), bs)
    pool_hist = jnp.pad(state_pool, ((0, 0), (0, 0), (POOL_ROWS - state_pool.shape[2], 0), (0, 0)))
    y_s, hg_s, pool_s, conv_s, cv_s = _trunk(
        x_sample.reshape(1, bs * dseq, D_MODEL), state_hgrn, pool_hist, state_ffn_conv, PAST_LEN,
        dict(common, w_mix=wm_s, c_bias=bias_s), nseq=bs, L=dseq, want_v=True)
    n_hist = state_pool.shape[2]
    cv_s = jnp.stack(cv_s).reshape(n_odd, bs, dseq, C_WIDTH)
    return (y_p, y_s.reshape(bs, dseq, D_MODEL), hg_p, hg_s, pool_p[:, :, POOL_ROWS - n_hist:],
            pool_s[:, :, POOL_ROWS - n_hist:], conv_p, conv_s, cv_s)
```
